```python
import math
import jax
import jax.numpy as jnp
from jax import lax
import numpy as np

D_MODEL = 1024
BATCH = 8
SEQ = 2048
DEPTH = 4
DEC_BATCH = 128
DEC_SEQ = 4
PAST_LEN = 2048
PAGE_SIZE = 128

N_MIXERS = 2
N_GDN = (DEPTH + N_MIXERS - 1) // N_MIXERS
N_SWA = DEPTH // N_MIXERS

GDN_QK_HEADS = 8
GDN_V_HEADS = 16
GDN_DK = 128
GDN_DV = 128
GDN_CONV_W = 4
GDN_CHUNK = 64
GDN_QK_DIM = GDN_QK_HEADS * GDN_DK
GDN_V_DIM = GDN_V_HEADS * GDN_DV
GDN_CONV_DIM = 2 * GDN_QK_DIM + GDN_V_DIM
GDN_IN_DIM = GDN_CONV_DIM + GDN_V_DIM + 2 * GDN_V_HEADS

SWA_WINDOWS = (128, 512, 2048)
SWA_DILATIONS = (1, 4, 16)
SWA_GROUPS = 3
SWA_HEADS = 16
SWA_HEAD_DIM = 64
SWA_KEYS = 128
SWA_INNER = SWA_HEADS * SWA_HEAD_DIM
SWA_IN_DIM = SWA_GROUPS * 3 * SWA_INNER

N_BUCKETS = 32
MAX_DISTANCE = 2048

D_FF = -(-8 * D_MODEL // (3 * 256)) * 256

DEEPNORM_ALPHA = (2 * DEPTH) ** 0.25
DEEPNORM_BETA = (8 * DEPTH) ** -0.25
LN_EPS = 1e-5
RMS_EPS = 1e-6
L2_EPS = 1e-6

kernel_name = 'hybrid_gdn_dilated_swa_step'


def layer_norm(x, g, b):
    xf = x.astype(jnp.float32)
    mu = jnp.mean(xf, -1, keepdims=True)
    var = jnp.mean(jnp.square(xf - mu), -1, keepdims=True)
    return ((xf - mu) * lax.rsqrt(var + LN_EPS) * g.astype(jnp.float32) + b.astype(jnp.float32)).astype(x.dtype)


def post_norm(x, h, g, b):
    return layer_norm(DEEPNORM_ALPHA * x + h, g, b)


def l2norm(x):
    return x * lax.rsqrt(jnp.sum(x * x, -1, keepdims=True) + L2_EPS)


def swiglu_ffn(x, w_in, w_out):
    h = jnp.einsum('btd,df->btf', x, w_in)
    return jnp.einsum('btf,fd->btd', jax.nn.silu(h[..., :D_FF]) * h[..., D_FF:], w_out)


def gdn_chunked(q, k, v, g, beta, s0):
    B, S, H, _ = q.shape
    C = GDN_CHUNK
    n = S // C
    q, k, v = [t.transpose(0, 2, 1, 3).reshape(B, H, n, C, t.shape[-1]) for t in (q, k, v)]
    g, beta = [t.transpose(0, 2, 1).reshape(B, H, n, C) for t in (g, beta)]
    gc = jnp.cumsum(g, axis=-1)
    tri = jnp.tril(jnp.ones((C, C), bool))
    strict = jnp.tril(jnp.ones((C, C), bool), -1)
    decay = jnp.exp(jnp.where(tri, gc[..., :, None] - gc[..., None, :], -jnp.inf))
    kb = k * beta[..., None]
    lmat = jnp.where(strict, jnp.einsum('bhnid,bhnjd->bhnij', kb, k) * decay, 0.0)
    amat = lmat + jnp.eye(C, dtype=lmat.dtype)
    rhs = jnp.concatenate([v * beta[..., None], kb * jnp.exp(gc)[..., None]], axis=-1)
    sol = lax.linalg.triangular_solve(amat, rhs, left_side=True, lower=True, unit_diagonal=True)
    u, w = sol[..., :GDN_DV], sol[..., GDN_DV:]
    a_intra = jnp.where(tri, jnp.einsum('bhnid,bhnjd->bhnij', q, k) * decay, 0.0)

    def step(state, xs):
        q_c, k_c, u_c, w_c, gc_c, a_c = xs
        v_new = u_c - jnp.einsum('bhcd,bhde->bhce', w_c, state)
        o = (jnp.einsum('bhcd,bhde->bhce', q_c * jnp.exp(gc_c)[..., None], state)
             + jnp.einsum('bhij,bhje->bhie', a_c, v_new))
        g_last = gc_c[..., -1]
        state = (state * jnp.exp(g_last)[..., None, None]
                 + jnp.einsum('bhcd,bhce->bhde', k_c * jnp.exp(g_last[..., None] - gc_c)[..., None], v_new))
        return state, o

    xs = tuple(jnp.moveaxis(t, 2, 0) for t in (q, k, u, w, gc, a_intra))
    s_fin, o = lax.scan(step, s0, xs)
    o = jnp.moveaxis(o, 0, 2).reshape(B, H, S, GDN_DV).transpose(0, 2, 1, 3)
    return o, s_fin


def gdn_recurrent(q, k, v, g, beta, s0):
    def step(state, xs):
        q_t, k_t, v_t, g_t, b_t = xs
        state = state * jnp.exp(g_t)[..., None, None]
        delta = (v_t - jnp.einsum('bhd,bhde->bhe', k_t, state)) * b_t[..., None]
        state = state + jnp.einsum('bhd,bhe->bhde', k_t, delta)
        return state, jnp.einsum('bhd,bhde->bhe', q_t, state)

    xs = tuple(jnp.moveaxis(t, 1, 0) for t in (q, k, v, g, beta))
    s_fin, o = lax.scan(step, s0, xs)
    return jnp.moveaxis(o, 0, 1), s_fin


def gdn_mixer(x, conv_buf, s0, w_in, conv_w, a_log, dt_bias, norm_w, w_out, chunked):
    B, T, _ = x.shape
    p = jnp.einsum('btd,de->bte', x, w_in)
    o1 = GDN_CONV_DIM
    o2 = o1 + GDN_V_DIM
    o3 = o2 + GDN_V_HEADS
    qkv, z, a, b = p[..., :o1], p[..., o1:o2], p[..., o2:o3], p[..., o3:]
    xc = jnp.concatenate([conv_buf.astype(qkv.dtype), qkv], axis=1)
    conv = sum(xc[:, w:w + T] * conv_w[w] for w in range(GDN_CONV_W))
    new_conv = xc[:, T:]
    qkv_c = jax.nn.silu(conv).astype(jnp.float32)
    q = qkv_c[..., :GDN_QK_DIM].reshape(B, T, GDN_QK_HEADS, GDN_DK)
    k = qkv_c[..., GDN_QK_DIM:2 * GDN_QK_DIM].reshape(B, T, GDN_QK_HEADS, GDN_DK)
    v = qkv_c[..., 2 * GDN_QK_DIM:].reshape(B, T, GDN_V_HEADS, GDN_DV)
    rep = GDN_V_HEADS // GDN_QK_HEADS
    q = jnp.repeat(l2norm(q) * GDN_DK ** -0.5, rep, axis=2)
    k = jnp.repeat(l2norm(k), rep, axis=2)
    g = -jnp.exp(a_log.astype(jnp.float32)) * jax.nn.softplus(a.astype(jnp.float32) + dt_bias.astype(jnp.float32))
    beta = jax.nn.sigmoid(b.astype(jnp.float32))
    s0 = s0.astype(jnp.float32)
    if chunked:
        o, s_fin = gdn_chunked(q, k, v, g, beta, s0)
    else:
        o, s_fin = gdn_recurrent(q, k, v, g, beta, s0)
    o = (o * lax.rsqrt(jnp.mean(o * o, -1, keepdims=True) + RMS_EPS) * norm_w.astype(jnp.float32)
         * jax.nn.silu(z.astype(jnp.float32).reshape(B, T, GDN_V_HEADS, GDN_DV)))
    y = jnp.einsum('bti,id->btd', o.reshape(B, T, GDN_V_DIM).astype(x.dtype), w_out)
    return y, new_conv, s_fin


def t5_bucket(dist):
    max_exact = N_BUCKETS // 2
    d = jnp.maximum(dist, 1).astype(jnp.float32)
    large = max_exact + (jnp.log(d / max_exact) / math.log(MAX_DISTANCE / max_exact)
                         * (N_BUCKETS - max_exact)).astype(jnp.int32)
    return jnp.where(dist < max_exact, dist, jnp.minimum(large, N_BUCKETS - 1))


def group_bias(rel_bias, g):
    dist = jnp.arange(SWA_KEYS + 1, dtype=jnp.int32) * SWA_DILATIONS[g]
    return rel_bias[t5_bucket(dist), g * SWA_HEADS:(g + 1) * SWA_HEADS].T.astype(jnp.float32)


def dilated_prompt(q, k, v, bias_g, d):
    B, S, H, E = q.shape
    blk = SWA_KEYS
    span = d * blk
    sp = -(-S // span) * span
    m_len = sp // d
    nb = m_len // blk

    def split(t):
        t = jnp.pad(t, ((0, 0), (0, sp - S), (0, 0), (0, 0)))
        return t.reshape(B, m_len, d, H, E).transpose(0, 2, 1, 3, 4).reshape(B, d, nb, blk, H, E)

    def band(tb):
        prev = jnp.pad(tb, ((0, 0), (0, 0), (1, 0), (0, 0), (0, 0), (0, 0)))[:, :, :nb]
        return jnp.concatenate([prev, tb], axis=3)

    qb = split(q) * SWA_HEAD_DIM ** -0.5
    kb, vb = band(split(k)), band(split(v))
    steps = blk + jnp.arange(blk)[:, None] - jnp.arange(2 * blk)[None, :]
    valid = (steps >= 0) & (steps <= SWA_KEYS)
    not_before_start = (jnp.arange(nb)[:, None, None] > 0) | (jnp.arange(2 * blk) >= blk)[None, None, :]
    valid = valid[None] & not_before_start
    bias = jnp.take(bias_g, jnp.clip(steps, 0, SWA_KEYS), axis=1)
    s = jnp.einsum('bdnihe,bdnjhe->bdnhij', qb, kb) + bias
    s = jnp.where(valid[:, None], s, -jnp.inf)
    m = jnp.max(s, -1, keepdims=True)
    p = jnp.exp(s - m)
    den = jnp.sum(p, -1)
    o = jnp.einsum('bdnhij,bdnjhe->bdnihe', p, vb) / jnp.swapaxes(den, -1, -2)[..., None]
    lse = jnp.swapaxes(m[..., 0] + jnp.log(den), -1, -2)
    o = o.reshape(B, d, m_len, H, E).transpose(0, 2, 1, 3, 4).reshape(B, sp, H, E)[:, :S]
    lse = lse.reshape(B, d, m_len, H).transpose(0, 2, 1, 3).reshape(B, sp, H)[:, :S]
    return o, lse


def dilated_sample(q, k, v, buf, bias_g, d):
    L = buf.shape[1]
    T = q.shape[1]
    idx = L + jnp.arange(T)[:, None] - d * jnp.arange(SWA_KEYS + 1)[None, :]
    valid = idx >= 0
    in_buf = idx < L
    new_kv = jnp.stack([k, v], axis=2)
    kv = jnp.where(in_buf[None, :, :, None, None, None],
                   buf[:, jnp.clip(idx, 0, L - 1)].astype(jnp.float32),
                   new_kv[:, jnp.clip(idx - L, 0, T - 1)])
    s = jnp.einsum('bthe,btjhe->bthj', q * SWA_HEAD_DIM ** -0.5, kv[:, :, :, 0]) + bias_g[None, None]
    s = jnp.where(valid[None, :, None, :], s, -jnp.inf)
    m = jnp.max(s, -1, keepdims=True)
    p = jnp.exp(s - m)
    den = jnp.sum(p, -1)
    o = jnp.einsum('bthj,btjhe->bthe', p, kv[:, :, :, 1]) / den[..., None]
    return o, m[..., 0] + jnp.log(den)


def merge_groups(outs, lses, w_out, dtype):
    wts = jax.nn.softmax(jnp.stack(lses, 0), axis=0)
    o = jnp.einsum('gbthe,gbth->bthe', jnp.stack(outs, 0), wts)
    B, T = o.shape[:2]
    return jnp.einsum('bti,id->btd', o.reshape(B, T, SWA_INNER).astype(dtype), w_out)


def swa_project(x, w_in):
    B, T, _ = x.shape
    return jnp.einsum('btd,de->bte', x, w_in).reshape(B, T, SWA_GROUPS, 3, SWA_HEADS, SWA_HEAD_DIM)


def swa_prompt(x, w_in, w_out, rel_bias):
    p = swa_project(x, w_in)
    outs, lses, rows = [], [], []
    for g in range(SWA_GROUPS):
        q, k, v = p[:, :, g, 0], p[:, :, g, 1], p[:, :, g, 2]
        o, lse = dilated_prompt(q.astype(jnp.float32), k.astype(jnp.float32), v.astype(jnp.float32),
                                group_bias(rel_bias, g), SWA_DILATIONS[g])
        outs.append(o)
        lses.append(lse)
        keep = min(SWA_WINDOWS[g], x.shape[1])
        rows.append(jnp.stack([k, v], axis=2)[:, -keep:])
    return merge_groups(outs, lses, w_out, x.dtype), rows


def swa_sample(x, bufs, w_in, w_out, rel_bias):
    p = swa_project(x, w_in)
    outs, lses, rows = [], [], []
    for g in range(SWA_GROUPS):
        q, k, v = p[:, :, g, 0], p[:, :, g, 1], p[:, :, g, 2]
        o, lse = dilated_sample(q.astype(jnp.float32), k.astype(jnp.float32), v.astype(jnp.float32),
                                bufs[g], group_bias(rel_bias, g), SWA_DILATIONS[g])
        outs.append(o)
        lses.append(lse)
        rows.append(jnp.stack([k, v], axis=2))
    return merge_groups(outs, lses, w_out, x.dtype), rows


def setup_inputs(seed: int = 0) -> dict:
    key = jax.random.key(seed)
    ks = jax.random.split(key, 24)
    nrm = jax.random.normal
    f32 = jnp.float32
    dt = jnp.exp(jax.random.uniform(ks[10], (N_GDN, GDN_V_HEADS), f32, math.log(1e-3), math.log(1e-1)))
    return {
        'x_prompt': nrm(ks[0], (BATCH, SEQ, D_MODEL), f32),
        'x_sample': nrm(ks[1], (DEC_BATCH, DEC_SEQ, D_MODEL), f32),
        'state_gdn': 0.1 * nrm(ks[2], (N_GDN, DEC_BATCH, GDN_V_HEADS, GDN_DK, GDN_DV), f32),
        'state_conv': nrm(ks[3], (N_GDN, DEC_BATCH, GDN_CONV_W - 1, GDN_CONV_DIM), f32),
        'cache_win0': nrm(ks[4], (N_SWA, DEC_BATCH, min(SWA_WINDOWS[0], PAST_LEN), 2, SWA_HEADS, SWA_HEAD_DIM), f32),
        'cache_win1': nrm(ks[5], (N_SWA, DEC_BATCH, min(SWA_WINDOWS[1], PAST_LEN), 2, SWA_HEADS, SWA_HEAD_DIM), f32),
        'cache_win2': nrm(ks[6], (N_SWA, DEC_BATCH, min(SWA_WINDOWS[2], PAST_LEN), 2, SWA_HEADS, SWA_HEAD_DIM), f32),
        'w_in_gdn': nrm(ks[7], (N_GDN, D_MODEL, GDN_IN_DIM), f32) * D_MODEL ** -0.5,
        'conv_w_gdn': nrm(ks[8], (N_GDN, GDN_CONV_W, GDN_CONV_DIM), f32) * GDN_CONV_W ** -0.5,
        'a_log_gdn': jnp.log(jax.random.uniform(ks[9], (N_GDN, GDN_V_HEADS), f32, 1.0, 16.0)),
        'dt_bias_gdn': dt + jnp.log(-jnp.expm1(-dt)),
        'norm_gdn': 1.0 + 0.02 * nrm(ks[11], (N_GDN, GDN_DV), f32),
        'w_out_gdn': nrm(ks[12], (N_GDN, GDN_V_DIM, D_MODEL), f32) * GDN_V_DIM ** -0.5 * DEEPNORM_BETA,
        'w_in_swa': nrm(ks[13], (N_SWA, D_MODEL, SWA_IN_DIM), f32) * D_MODEL ** -0.5,
        'w_out_swa': nrm(ks[14], (N_SWA, SWA_INNER, D_MODEL), f32) * SWA_INNER ** -0.5 * DEEPNORM_BETA,
        'rel_bias': 0.1 * nrm(ks[15], (N_BUCKETS, SWA_GROUPS * SWA_HEADS), f32),
        'w_ffn_in': nrm(ks[16], (DEPTH, D_MODEL, 2 * D_FF), f32) * D_MODEL ** -0.5,
        'w_ffn_out': nrm(ks[17], (DEPTH, D_FF, D_MODEL), f32) * D_FF ** -0.5 * DEEPNORM_BETA,
        'ln_mix_g': 1.0 + 0.02 * nrm(ks[18], (DEPTH, D_MODEL), f32),
        'ln_mix_b': 0.02 * nrm(ks[19], (DEPTH, D_MODEL), f32),
        'ln_ffn_g': 1.0 + 0.02 * nrm(ks[20], (DEPTH, D_MODEL), f32),
        'ln_ffn_b': 0.02 * nrm(ks[21], (DEPTH, D_MODEL), f32),
    }


def reference(x_prompt, x_sample, state_gdn, state_conv, cache_win0, cache_win1, cache_win2,
              w_in_gdn, conv_w_gdn, a_log_gdn, dt_bias_gdn, norm_gdn, w_out_gdn,
              w_in_swa, w_out_swa, rel_bias, w_ffn_in, w_ffn_out,
              ln_mix_g, ln_mix_b, ln_ffn_g, ln_ffn_b):
    caches = (cache_win0, cache_win1, cache_win2)
    xp, xs = x_prompt, x_sample
    gdn_p, gdn_s, conv_p, conv_s = [], [], [], []
    win_p = [[] for _ in range(SWA_GROUPS)]
    win_s = [[] for _ in range(SWA_GROUPS)]
    for i in range(DEPTH):
        j = i // N_MIXERS
        if i % N_MIXERS == 0:
            prm = (w_in_gdn[j], conv_w_gdn[j], a_log_gdn[j], dt_bias_gdn[j], norm_gdn[j], w_out_gdn[j])
            conv0 = jnp.zeros((xp.shape[0], GDN_CONV_W - 1, GDN_CONV_DIM), xp.dtype)
            s0 = jnp.zeros((xp.shape[0], GDN_V_HEADS, GDN_DK, GDN_DV), jnp.float32)
            hp, cp, sp = gdn_mixer(xp, conv0, s0, *prm, chunked=True)
            hs, cs, ss = gdn_mixer(xs, state_conv[j], state_gdn[j], *prm, chunked=False)
            gdn_p.append(sp)
            gdn_s.append(ss)
            conv_p.append(cp)
            conv_s.append(cs)
        else:
            hp, rows_p = swa_prompt(xp, w_in_swa[j], w_out_swa[j], rel_bias)
            hs, rows_s = swa_sample(xs, tuple(c[j] for c in caches), w_in_swa[j], w_out_swa[j], rel_bias)
            for g in range(SWA_GROUPS):
                win_p[g].append(rows_p[g])
                win_s[g].append(rows_s[g])
        xp = post_norm(xp, hp, ln_mix_g[i], ln_mix_b[i])
        xs = post_norm(xs, hs, ln_mix_g[i], ln_mix_b[i])
        xp = post_norm(xp, swiglu_ffn(xp, w_ffn_in[i], w_ffn_out[i]), ln_ffn_g[i], ln_ffn_b[i])
        xs = post_norm(xs, swiglu_ffn(xs, w_ffn_in[i], w_ffn_out[i]), ln_ffn_g[i], ln_ffn_b[i])
    new_state_gdn_p = jnp.stack(gdn_p)
    new_state_conv_p = jnp.stack(conv_p)
    new_win0_p = jnp.stack(win_p[0])
    new_win1_p = jnp.stack(win_p[1])
    new_win2_p = jnp.stack(win_p[2])
    new_state_gdn_s = jnp.stack(gdn_s)
    new_state_conv_s = jnp.stack(conv_s)
    new_win0_s = jnp.stack(win_s[0])
    new_win1_s = jnp.stack(win_s[1])
    new_win2_s = jnp.stack(win_s[2])
    return (xp, xs, new_state_gdn_p, new_state_conv_p, new_win0_p, new_win1_p, new_win2_p,
            new_state_gdn_s, new_state_conv_s, new_win0_s, new_win1_s, new_win2_s)
```

```python
import functools
import math

import jax
import jax.numpy as jnp
from jax import lax
from jax.experimental import pallas as pl
from jax.experimental.pallas import tpu as pltpu

f32 = jnp.float32
bf16 = jnp.bfloat16

D_MODEL = 1024
DEPTH = 4
N_MIXERS = 2
GDN_QK_HEADS = 8
GDN_V_HEADS = 16
GDN_DK = 128
GDN_DV = 128
GDN_CONV_W = 4
GDN_QK_DIM = GDN_QK_HEADS * GDN_DK
GDN_V_DIM = GDN_V_HEADS * GDN_DV
GDN_CONV_DIM = 2 * GDN_QK_DIM + GDN_V_DIM
GDN_QKVZ_DIM = GDN_CONV_DIM + GDN_V_DIM
SWA_DILATIONS = (1, 4, 16)
SWA_GROUPS = 3
SWA_HEADS = 16
SWA_HEAD_DIM = 64
SWA_KEYS = 128
SWA_INNER = SWA_HEADS * SWA_HEAD_DIM
SWA_IN_DIM = SWA_GROUPS * 3 * SWA_INNER
N_BUCKETS = 32
MAX_DISTANCE = 2048
D_FF = 2816
DEEPNORM_ALPHA = (2 * DEPTH) ** 0.25
LN_EPS = 1e-5
RMS_EPS = 1e-6
L2_EPS = 1e-6

LANES = 128
SUBLANES = 8
VMEM_LIMIT = 56 * 1024 * 1024

CHUNK = 128
SAMPLE_ROWS = 8
FFN_TILE = 256


def _cparams(*sem):
    return pltpu.CompilerParams(dimension_semantics=sem, vmem_limit_bytes=VMEM_LIMIT)


def _dot(a, b):
    return jnp.dot(a, b, preferred_element_type=f32)


def _dot_split(a, b):
    ah = a.astype(bf16)
    al = (a - ah.astype(f32)).astype(bf16)
    bh = b.astype(bf16)
    bl = (b - bh.astype(f32)).astype(bf16)
    return _dot(ah, bh) + (_dot(ah, bl) + _dot(al, bh))


def _dot_nt(a, b):
    return lax.dot_general(a, b, (((1,), (1,)), ((), ())), preferred_element_type=f32)


def _sigmoid(x):
    return 1.0 / (1.0 + jnp.exp(-x))


def _silu(x):
    return x * _sigmoid(x)


def _layer_norm(y, g, b):
    mu = jnp.mean(y, axis=-1, keepdims=True)
    yc = y - mu
    var = jnp.mean(yc * yc, axis=-1, keepdims=True)
    return yc * lax.rsqrt(var + LN_EPS) * g + b


def _proj_body(x_ref, w_ref, o_ref, xb_ref):
    @pl.when(pl.program_id(1) == 0)
    def _():
        xb_ref[...] = x_ref[...].astype(bf16)

    o_ref[...] = _dot(xb_ref[...], w_ref[...]).astype(o_ref.dtype)


def _proj(x, w, tm, tn, name):
    m, k = x.shape
    n = w.shape[1]
    return pl.pallas_call(
        _proj_body,
        out_shape=jax.ShapeDtypeStruct((m, n), f32),
        grid=(m // tm, n // tn),
        in_specs=[pl.BlockSpec((tm, k), lambda i, j: (i, 0)),
                  pl.BlockSpec((k, tn), lambda i, j: (0, j))],
        out_specs=pl.BlockSpec((tm, tn), lambda i, j: (i, j)),
        scratch_shapes=[pltpu.VMEM((tm, k), bf16)],
        compiler_params=_cparams("parallel", "arbitrary"),
        name=name,
    )(x, w)


def _mm_ln_body(a_ref, w_ref, x_ref, g_ref, b_ref, o_ref):
    h = _dot(a_ref[...].astype(bf16), w_ref[...])
    o_ref[...] = _layer_norm(DEEPNORM_ALPHA * x_ref[...] + h, g_ref[...], b_ref[...])


def _mm_ln(a, w, x, g, b, tm, name):
    m, k = a.shape
    return pl.pallas_call(
        _mm_ln_body,
        out_shape=jax.ShapeDtypeStruct((m, D_MODEL), f32),
        grid=(m // tm,),
        in_specs=[pl.BlockSpec((tm, k), lambda i: (i, 0)),
                  pl.BlockSpec((k, D_MODEL), lambda i: (0, 0)),
                  pl.BlockSpec((tm, D_MODEL), lambda i: (i, 0)),
                  pl.BlockSpec((1, D_MODEL), lambda i: (0, 0)),
                  pl.BlockSpec((1, D_MODEL), lambda i: (0, 0))],
        out_specs=pl.BlockSpec((tm, D_MODEL), lambda i: (i, 0)),
        compiler_params=_cparams("parallel"),
        name=name,
    )(a, w, x, g, b)


def _ffn_body(x_ref, w1_ref, w2_ref, wo_ref, g_ref, b_ref, o_ref, xb_ref, acc_ref):
    j = pl.program_id(1)

    @pl.when(j == 0)
    def _():
        xb_ref[...] = x_ref[...].astype(bf16)
        acc_ref[...] = jnp.zeros_like(acc_ref)

    xb = xb_ref[...]
    act = _silu(_dot(xb, w1_ref[...])) * _dot(xb, w2_ref[...])
    acc_ref[...] += _dot(act.astype(bf16), wo_ref[...])

    @pl.when(j == pl.num_programs(1) - 1)
    def _():
        o_ref[...] = _layer_norm(DEEPNORM_ALPHA * x_ref[...] + acc_ref[...], g_ref[...], b_ref[...])


def _ffn(x, w_in, w_out, g, b, tm, name):
    m = x.shape[0]
    nf = D_FF // FFN_TILE
    return pl.pallas_call(
        _ffn_body,
        out_shape=jax.ShapeDtypeStruct((m, D_MODEL), f32),
        grid=(m // tm, nf),
        in_specs=[pl.BlockSpec((tm, D_MODEL), lambda i, j: (i, 0)),
                  pl.BlockSpec((D_MODEL, FFN_TILE), lambda i, j: (0, j)),
                  pl.BlockSpec((D_MODEL, FFN_TILE), lambda i, j: (0, j + nf)),
                  pl.BlockSpec((FFN_TILE, D_MODEL), lambda i, j: (j, 0)),
                  pl.BlockSpec((1, D_MODEL), lambda i, j: (0, 0)),
                  pl.BlockSpec((1, D_MODEL), lambda i, j: (0, 0))],
        out_specs=pl.BlockSpec((tm, D_MODEL), lambda i, j: (i, 0)),
        scratch_shapes=[pltpu.VMEM((tm, D_MODEL), bf16), pltpu.VMEM((tm, D_MODEL), f32)],
        compiler_params=_cparams("parallel", "arbitrary"),
        name=name,
    )(x, w_in, w_in, w_out, g, b)


def _gates_body(x_ref, w_ref, al_ref, dt_ref, gb_ref, gbt_ref):
    ab = _dot(x_ref[...].astype(bf16), w_ref[...])
    a = ab + dt_ref[...]
    softplus = jnp.maximum(a, 0.0) + jnp.log1p(jnp.exp(-jnp.abs(a)))
    g = -jnp.exp(al_ref[...]) * softplus
    beta = _sigmoid(ab)
    tm = ab.shape[0]
    row = lax.broadcasted_iota(jnp.int32, (CHUNK, CHUNK), 0)
    col = lax.broadcasted_iota(jnp.int32, (CHUNK, CHUNK), 1)
    tril = (row >= col).astype(f32)
    gc = jnp.concatenate(
        [jnp.dot(tril, g[c * CHUNK:(c + 1) * CHUNK], precision=lax.Precision.HIGHEST,
                 preferred_element_type=f32) for c in range(tm // CHUNK)], axis=0)
    lane = lax.broadcasted_iota(jnp.int32, (1, LANES), 1)
    gb = jnp.where(lane < GDN_V_HEADS, gc, jnp.where(lane < 2 * GDN_V_HEADS, beta, g))
    gb_ref[...] = gb
    gbt_ref[...] = gb.T


def _gates(x, w_ab, alog, dtb, tm, name):
    m = x.shape[0]
    return pl.pallas_call(
        _gates_body,
        out_shape=(jax.ShapeDtypeStruct((m, LANES), f32), jax.ShapeDtypeStruct((LANES, m), f32)),
        grid=(m // tm,),
        in_specs=[pl.BlockSpec((tm, D_MODEL), lambda i: (i, 0)),
                  pl.BlockSpec((D_MODEL, LANES), lambda i: (0, 0)),
                  pl.BlockSpec((1, LANES), lambda i: (0, 0)),
                  pl.BlockSpec((1, LANES), lambda i: (0, 0))],
        out_specs=(pl.BlockSpec((tm, LANES), lambda i: (i, 0)),
                   pl.BlockSpec((LANES, tm), lambda i: (0, i))),
        compiler_params=_cparams("parallel"),
        name=name,
    )(x, w_ab, alog, dtb)


def _lane_column(x, lane, idx):
    return jnp.sum(jnp.where(lane == idx, x, 0.0), axis=1, keepdims=True)


def _l2norm(x):
    return x * lax.rsqrt(jnp.sum(x * x, axis=-1, keepdims=True) + L2_EPS)


def _gated_rmsnorm(o, nw, z):
    return o * lax.rsqrt(jnp.mean(o * o, axis=-1, keepdims=True) + RMS_EPS) * nw * _silu(z)


def _gdn_prompt_body(q_ref, k_ref, v_ref, z_ref, gb_ref, gbt_ref, cwq_ref, cwk_ref, cwv_ref, nw_ref,
                     o_ref, st_ref, s_scr):
    hq = pl.program_id(1)
    seq = q_ref.shape[0]
    s_scr[...] = jnp.zeros_like(s_scr)
    row = lax.broadcasted_iota(jnp.int32, (CHUNK, CHUNK), 0)
    col = lax.broadcasted_iota(jnp.int32, (CHUNK, CHUNK), 1)
    tri = row >= col
    strict = row > col
    eye = (row == col).astype(f32)
    lane = lax.broadcasted_iota(jnp.int32, (1, LANES), 1)
    rows8 = lax.broadcasted_iota(jnp.int32, (SUBLANES, 1), 0)
    cwq = cwq_ref[...]
    cwk = cwk_ref[...]
    cwv = cwv_ref[...]
    nw = nw_ref[...]

    def conv_silu(x, prev8, cw):
        acc = x * cw[GDN_CONV_W - 1:GDN_CONV_W]
        for s in range(1, GDN_CONV_W):
            rolled = pltpu.roll(x, s, 0)
            top = jnp.where(rows8 < s, pltpu.roll(prev8, s, 0), rolled[:SUBLANES])
            shifted = jnp.concatenate([top, rolled[SUBLANES:]], axis=0)
            acc = acc + shifted * cw[GDN_CONV_W - 1 - s:GDN_CONV_W - s]
        return _silu(acc)

    def chunk(c, carry):
        pq, pk, pv = carry
        r0 = pl.multiple_of(c * CHUNK, CHUNK)
        xq = q_ref[pl.ds(r0, CHUNK), :]
        xk = k_ref[pl.ds(r0, CHUNK), :]
        xv = v_ref[pl.ds(r0, CHUNK), :]
        qn = _l2norm(conv_silu(xq, pq, cwq)) * GDN_DK ** -0.5
        kn = _l2norm(conv_silu(xk, pk, cwk))
        vc = conv_silu(xv, pv, cwv)
        qb = qn.astype(bf16)
        kb = kn.astype(bf16)
        kk = _dot_nt(kb, kb)
        qk = _dot_nt(qb, kb)
        k_t = kn.T
        gbc = gb_ref[pl.ds(r0, CHUNK), :]
        zc = z_ref[pl.ds(r0, CHUNK), :]
        tile0 = pl.multiple_of((2 * hq) // SUBLANES * SUBLANES, SUBLANES)
        gbt8 = gbt_ref[pl.ds(tile0, SUBLANES), pl.ds(r0, CHUNK)]
        outs = []
        for hh in range(2):
            hv = 2 * hq + hh
            gc_col = _lane_column(gbc, lane, hv)
            beta_col = _lane_column(gbc, lane, GDN_V_HEADS + hv)
            gc_row = jnp.sum(jnp.where(rows8 == hv % SUBLANES, gbt8, 0.0), axis=0, keepdims=True)
            decay = jnp.where(tri, jnp.exp(jnp.where(tri, gc_col - gc_row, 0.0)), 0.0)
            n_pow = jnp.where(strict, -(beta_col * kk * decay), 0.0)
            a_inv = eye + n_pow
            for _ in range(6):
                n_pow = _dot_split(n_pow, n_pow)
                a_inv = a_inv + _dot_split(a_inv, n_pow)
            a_inv_b = a_inv.astype(bf16)
            a_intra = jnp.where(tri, qk * decay, 0.0)
            vh = vc[:, hh * GDN_DV:(hh + 1) * GDN_DV]
            u = _dot(a_inv_b, (vh * beta_col).astype(bf16))
            w = _dot(a_inv_b, (kn * (beta_col * jnp.exp(gc_col))).astype(bf16))
            state = s_scr[hh]
            state_b = state.astype(bf16)
            v_new = u - _dot(w.astype(bf16), state_b)
            v_new_b = v_new.astype(bf16)
            o = _dot((qn * jnp.exp(gc_col)).astype(bf16), state_b) + _dot(a_intra.astype(bf16), v_new_b)
            g_last = gc_row[:, CHUNK - 1:CHUNK]
            kd_t = k_t * jnp.exp(g_last - gc_row)
            s_scr[hh] = state * jnp.exp(g_last) + _dot(kd_t.astype(bf16), v_new_b)
            outs.append(_gated_rmsnorm(o, nw, zc[:, hh * GDN_DV:(hh + 1) * GDN_DV]))
        o_ref[pl.ds(r0, CHUNK), :] = jnp.concatenate(outs, axis=1).astype(o_ref.dtype)
        return xq[CHUNK - SUBLANES:], xk[CHUNK - SUBLANES:], xv[CHUNK - SUBLANES:]

    zeros8 = jnp.zeros((SUBLANES, GDN_DK), f32)
    lax.fori_loop(0, seq // CHUNK, chunk, (zeros8, zeros8, jnp.zeros((SUBLANES, 2 * GDN_DV), f32)))
    st_ref[0] = s_scr[...]


def _gdn_prompt(p, gb, gbt, conv_w, norm_w, batch, seq, name):
    nqk = GDN_QK_HEADS
    zoff = GDN_CONV_DIM // (2 * GDN_DV)
    return pl.pallas_call(
        _gdn_prompt_body,
        out_shape=(jax.ShapeDtypeStruct((batch * seq, GDN_V_DIM), bf16),
                   jax.ShapeDtypeStruct((batch, GDN_V_HEADS, GDN_DK, GDN_DV), f32)),
        grid=(batch, nqk),
        in_specs=[pl.BlockSpec((seq, GDN_DK), lambda b, h: (b, h)),
                  pl.BlockSpec((seq, GDN_DK), lambda b, h: (b, nqk + h)),
                  pl.BlockSpec((seq, 2 * GDN_DV), lambda b, h: (b, nqk + h)),
                  pl.BlockSpec((seq, 2 * GDN_DV), lambda b, h: (b, zoff + h)),
                  pl.BlockSpec((seq, LANES), lambda b, h: (b, 0)),
                  pl.BlockSpec((LANES, seq), lambda b, h: (0, b)),
                  pl.BlockSpec((GDN_CONV_W, GDN_DK), lambda b, h: (0, h)),
                  pl.BlockSpec((GDN_CONV_W, GDN_DK), lambda b, h: (0, nqk + h)),
                  pl.BlockSpec((GDN_CONV_W, 2 * GDN_DV), lambda b, h: (0, nqk + h)),
                  pl.BlockSpec((1, GDN_DV), lambda b, h: (0, 0))],
        out_specs=(pl.BlockSpec((seq, 2 * GDN_DV), lambda b, h: (b, h)),
                   pl.BlockSpec((1, 2, GDN_DK, GDN_DV), lambda b, h: (b, h, 0, 0))),
        scratch_shapes=[pltpu.VMEM((2, GDN_DK, GDN_DV), f32)],
        compiler_params=_cparams("parallel", "parallel"),
        name=name,
    )(p, p, p, p, gb, gbt, conv_w, conv_w, conv_w, norm_w)


def _gdn_sample_body(q_ref, k_ref, v_ref, z_ref, aq_ref, ak_ref, av_ref, gb_ref, cwq_ref, cwk_ref, cwv_ref,
                     nw_ref, st_ref, o_ref, sto_ref, q_scr, k_scr, v_scr, o_scr):
    hq = pl.program_id(1)
    rows = q_ref.shape[0]
    nreq = rows // SAMPLE_ROWS
    tok = lax.broadcasted_iota(jnp.int32, (rows, 1), 0) % SAMPLE_ROWS
    lane = lax.broadcasted_iota(jnp.int32, (1, LANES), 1)
    rows8 = lax.broadcasted_iota(jnp.int32, (SAMPLE_ROWS, 1), 0)
    nw = nw_ref[...]
    hist = GDN_CONV_W - 1

    def conv_silu(a, x, cw):
        acc = jnp.zeros_like(x)
        for w in range(GDN_CONV_W):
            ra = a if w == 0 else pltpu.roll(a, rows - w, 0)
            rx = x if w == hist else pltpu.roll(x, hist - w, 0)
            acc = acc + jnp.where(tok + w < hist, ra, rx) * cw[w:w + 1]
        return _silu(acc)

    q_scr[...] = _l2norm(conv_silu(aq_ref[...], q_ref[...], cwq_ref[...])) * GDN_DK ** -0.5
    k_scr[...] = _l2norm(conv_silu(ak_ref[...], k_ref[...], cwk_ref[...]))
    v_scr[...] = conv_silu(av_ref[...], v_ref[...], cwv_ref[...])
    pad = jnp.zeros((LANES - 2 * SAMPLE_ROWS, GDN_DK), f32)

    def request(bi, carry):
        r0 = pl.multiple_of(bi * SAMPLE_ROWS, SAMPLE_ROWS)
        k8 = k_scr[pl.ds(r0, SAMPLE_ROWS), :]
        q8 = q_scr[pl.ds(r0, SAMPLE_ROWS), :]
        v8 = v_scr[pl.ds(r0, SAMPLE_ROWS), :]
        z8 = z_ref[pl.ds(r0, SAMPLE_ROWS), :]
        gb8 = gb_ref[pl.ds(r0, SAMPLE_ROWS), :]
        kq_t = jnp.concatenate([k8, q8, pad], axis=0).T
        outs = []
        for hh in range(2):
            hv = 2 * hq + hh
            g_col = _lane_column(gb8, lane, 2 * GDN_V_HEADS + hv)
            beta_col = _lane_column(gb8, lane, GDN_V_HEADS + hv)
            state = st_ref[bi, hh]
            o_acc = jnp.zeros((SAMPLE_ROWS, GDN_DV), f32)
            for t in range(GDN_CONV_W):
                k_col = kq_t[:, t:t + 1]
                q_col = kq_t[:, SAMPLE_ROWS + t:SAMPLE_ROWS + t + 1]
                state = state * jnp.exp(g_col[t:t + 1])
                ks = jnp.sum(state * k_col, axis=0, keepdims=True)
                delta = (v8[t:t + 1, hh * GDN_DV:(hh + 1) * GDN_DV] - ks) * beta_col[t:t + 1]
                state = state + k_col * delta
                o_t = jnp.sum(state * q_col, axis=0, keepdims=True)
                o_acc = jnp.where(rows8 == t, o_t, o_acc)
            sto_ref[bi, hh] = state
            outs.append(_gated_rmsnorm(o_acc, nw, z8[:, hh * GDN_DV:(hh + 1) * GDN_DV]))
        o_scr[pl.ds(r0, SAMPLE_ROWS), :] = jnp.concatenate(outs, axis=1)
        return carry

    lax.fori_loop(0, nreq, request, 0)
    o_ref[...] = o_scr[...].astype(o_ref.dtype)


def _gdn_sample(p, conv_buf, gb, conv_w, norm_w, state, layer, nreq_blk, name):
    assert GDN_CONV_W == 4 and SAMPLE_ROWS == 8
    nqk = GDN_QK_HEADS
    zoff = GDN_CONV_DIM // (2 * GDN_DV)
    nreq = state.shape[1]
    rows = nreq_blk * SAMPLE_ROWS
    st_spec = pl.BlockSpec((None, nreq_blk, 2, GDN_DK, GDN_DV), lambda i, h: (layer, i, h, 0, 0))
    return pl.pallas_call(
        _gdn_sample_body,
        out_shape=(jax.ShapeDtypeStruct((nreq * SAMPLE_ROWS, GDN_V_DIM), bf16),
                   jax.ShapeDtypeStruct((nreq, GDN_V_HEADS, GDN_DK, GDN_DV), f32)),
        grid=(nreq // nreq_blk, nqk),
        in_specs=[pl.BlockSpec((rows, GDN_DK), lambda i, h: (i, h)),
                  pl.BlockSpec((rows, GDN_DK), lambda i, h: (i, nqk + h)),
                  pl.BlockSpec((rows, 2 * GDN_DV), lambda i, h: (i, nqk + h)),
                  pl.BlockSpec((rows, 2 * GDN_DV), lambda i, h: (i, zoff + h)),
                  pl.BlockSpec((rows, GDN_DK), lambda i, h: (i, h)),
                  pl.BlockSpec((rows, GDN_DK), lambda i, h: (i, nqk + h)),
                  pl.BlockSpec((rows, 2 * GDN_DV), lambda i, h: (i, nqk + h)),
                  pl.BlockSpec((rows, LANES), lambda i, h: (i, 0)),
                  pl.BlockSpec((GDN_CONV_W, GDN_DK), lambda i, h: (0, h)),
                  pl.BlockSpec((GDN_CONV_W, GDN_DK), lambda i, h: (0, nqk + h)),
                  pl.BlockSpec((GDN_CONV_W, 2 * GDN_DV), lambda i, h: (0, nqk + h)),
                  pl.BlockSpec((1, GDN_DV), lambda i, h: (0, 0)),
                  st_spec],
        out_specs=(pl.BlockSpec((rows, 2 * GDN_DV), lambda i, h: (i, h)),
                   pl.BlockSpec((nreq_blk, 2, GDN_DK, GDN_DV), lambda i, h: (i, h, 0, 0))),
        scratch_shapes=[pltpu.VMEM((rows, GDN_DK), f32), pltpu.VMEM((rows, GDN_DK), f32),
                        pltpu.VMEM((rows, 2 * GDN_DV), f32), pltpu.VMEM((rows, 2 * GDN_DV), f32)],
        compiler_params=_cparams("parallel", "parallel"),
        name=name,
    )(p, p, p, p, conv_buf, conv_buf, conv_buf, gb, conv_w, conv_w, conv_w, norm_w, state)


def _swa_prompt_body(*refs):
    qkv_refs = refs[:3 * SWA_GROUPS]
    bias_ref, o_ref, o_scr, l_scr = refs[3 * SWA_GROUPS:]
    seq = o_ref.shape[0]
    lane = lax.broadcasted_iota(jnp.int32, (1, LANES), 1)
    head0 = lane < SWA_HEAD_DIM
    blk = SWA_KEYS
    for g, d in enumerate(SWA_DILATIONS):
        q_ref, k_ref, v_ref = qkv_refs[3 * g:3 * g + 3]
        nb = seq // (d * blk)

        def rows_of(start, d=d):
            return pl.ds(start, blk) if d == 1 else pl.ds(start, blk, stride=d)

        def block(idx, carry, g=g, d=d, nb=nb, q_ref=q_ref, k_ref=k_ref, v_ref=v_ref, rows_of=rows_of):
            r = idx // nb
            b = idx % nb
            cur = rows_of(r + d * blk * b)
            prev = rows_of(r + d * blk * jnp.maximum(b - 1, 0))
            no_prev = jnp.where(b == 0, -jnp.inf, 0.0).astype(f32)
            q = q_ref[cur, :] * SWA_HEAD_DIM ** -0.5
            kc = k_ref[cur, :].astype(bf16)
            kp = k_ref[prev, :].astype(bf16)
            vc = v_ref[cur, :].astype(bf16)
            vp = v_ref[prev, :].astype(bf16)
            o_h, l_h = [], []
            for hh in range(2):
                qh = jnp.where(head0 if hh == 0 else ~head0, q, 0.0).astype(bf16)
                bias = bias_ref[g, hh]
                s_c = _dot_nt(qh, kc) + bias[:, blk:]
                s_p = _dot_nt(qh, kp) + (bias[:, :blk] + no_prev)
                m = jnp.maximum(jnp.max(s_c, axis=-1, keepdims=True), jnp.max(s_p, axis=-1, keepdims=True))
                p_c = jnp.exp(s_c - m)
                p_p = jnp.exp(s_p - m)
                den = jnp.sum(p_c, axis=-1, keepdims=True) + jnp.sum(p_p, axis=-1, keepdims=True)
                pv = _dot(p_c.astype(bf16), vc) + _dot(p_p.astype(bf16), vp)
                o_h.append(pv / den)
                l_h.append(m + jnp.log(den))
            o_scr[g, cur, :] = jnp.where(head0, o_h[0], o_h[1])
            l_scr[g, cur, :] = jnp.where(head0, l_h[0], l_h[1])
            return carry

        lax.fori_loop(0, seq // blk, block, 0)

    def merge(c, carry):
        rr = pl.ds(pl.multiple_of(c * blk, blk), blk)
        ls = [l_scr[g, rr, :] for g in range(SWA_GROUPS)]
        mx = jnp.maximum(jnp.maximum(ls[0], ls[1]), ls[2])
        ws = [jnp.exp(l - mx) for l in ls]
        num = ws[0] * o_scr[0, rr, :] + ws[1] * o_scr[1, rr, :] + ws[2] * o_scr[2, rr, :]
        o_ref[rr, :] = (num / (ws[0] + ws[1] + ws[2])).astype(o_ref.dtype)
        return carry

    lax.fori_loop(0, seq // blk, merge, 0)


def _swa_prompt(p, bias_t, batch, seq, name):
    npair = SWA_INNER // LANES
    in_specs = []
    for g in range(SWA_GROUPS):
        for c in range(3):
            off = (g * 3 + c) * npair
            in_specs.append(pl.BlockSpec((seq, LANES), lambda b, h, off=off: (b, off + h)))
    in_specs.append(pl.BlockSpec((SWA_GROUPS, 2, SWA_KEYS, 2 * SWA_KEYS), lambda b, h: (0, h, 0, 0)))
    return pl.pallas_call(
        _swa_prompt_body,
        out_shape=jax.ShapeDtypeStruct((batch * seq, SWA_INNER), bf16),
        grid=(batch, npair),
        in_specs=in_specs,
        out_specs=pl.BlockSpec((seq, LANES), lambda b, h: (b, h)),
        scratch_shapes=[pltpu.VMEM((SWA_GROUPS, seq, LANES), f32), pltpu.VMEM((SWA_GROUPS, seq, LANES), f32)],
        compiler_params=_cparams("parallel", "parallel"),
        name=name,
    )(*([p] * (3 * SWA_GROUPS)), bias_t)


def _swa_sample_body(p_ref, c0_ref, c1_ref, c2_ref, tb_ref, bz_ref, o_ref):
    ntok = tb_ref.shape[1]
    head_of_lane = lax.broadcasted_iota(jnp.int32, (SWA_HEADS, SWA_INNER), 1) // SWA_HEAD_DIM
    own = head_of_lane == lax.broadcasted_iota(jnp.int32, (SWA_HEADS, SWA_INNER), 0)
    rows8 = lax.broadcasted_iota(jnp.int32, (SAMPLE_ROWS, 1), 0)
    kv_w = 2 * SWA_INNER
    cache_refs = (c0_ref, c1_ref, c2_ref)
    out = jnp.zeros((SAMPLE_ROWS, SWA_INNER), f32)
    for t in range(ntok):
        o_g, l_g = [], []
        for g, d in enumerate(SWA_DILATIONS):
            base = g * 3 * SWA_INNER
            c_ref = cache_refs[g]
            off = 0 if d == 1 else t * kv_w
            k_buf = c_ref[:, off:off + SWA_INNER].astype(bf16)
            v_buf = c_ref[:, off + SWA_INNER:off + kv_w].astype(bf16)
            q_row = p_ref[t:t + 1, base:base + SWA_INNER] * SWA_HEAD_DIM ** -0.5
            q_blk = jnp.where(own, q_row, 0.0)
            s = _dot_nt(q_blk.astype(bf16), k_buf) + tb_ref[g, t]
            m = jnp.max(s, axis=-1, keepdims=True)
            new = []
            for tn in (range(t + 1) if d == 1 else (t,)):
                k_new = p_ref[tn:tn + 1, base + SWA_INNER:base + 2 * SWA_INNER]
                j = (t - tn) // d
                s_n = jnp.sum(q_blk * k_new, axis=-1, keepdims=True) + bz_ref[g][:, j:j + 1]
                m = jnp.maximum(m, s_n)
                new.append((s_n, tn))
            p_buf = jnp.exp(s - m)
            den = jnp.sum(p_buf, axis=-1, keepdims=True)
            pv = _dot(p_buf.astype(bf16), v_buf)
            for s_n, tn in new:
                p_n = jnp.exp(s_n - m)
                den = den + p_n
                pv = pv + p_n * p_ref[tn:tn + 1, base + 2 * SWA_INNER:base + 3 * SWA_INNER]
            o_g.append(pv / den)
            l_g.append(m + jnp.log(den))
        mx = jnp.maximum(jnp.maximum(l_g[0], l_g[1]), l_g[2])
        ws = [jnp.exp(l - mx) for l in l_g]
        merged = (ws[0] * o_g[0] + ws[1] * o_g[1] + ws[2] * o_g[2]) / (ws[0] + ws[1] + ws[2])
        o_row = jnp.sum(jnp.where(own, merged, 0.0), axis=0, keepdims=True)
        out = jnp.where(rows8 == t, o_row, out)
    o_ref[...] = out.astype(o_ref.dtype)


def _swa_sample(p, caches, layer, tb, bz, name):
    nreq = p.shape[0] // SAMPLE_ROWS
    ntok = tb.shape[1]
    kv_w = 2 * SWA_INNER
    in_specs = [pl.BlockSpec((SAMPLE_ROWS, SWA_IN_DIM), lambda b: (b, 0))]
    views = []
    for c, d in zip(caches, SWA_DILATIONS):
        nl, nb, length = c.shape[:3]
        assert nb == nreq and length == SWA_KEYS * d and (d == 1 or ntok <= d)
        views.append(c.reshape(nl, nb, SWA_KEYS, d * kv_w))
        width = kv_w if d == 1 else ntok * kv_w
        in_specs.append(pl.BlockSpec((None, None, SWA_KEYS, width), lambda b: (layer, b, 0, 0)))
    in_specs.append(pl.BlockSpec(tb.shape, lambda b: (0, 0, 0, 0)))
    in_specs.append(pl.BlockSpec(bz.shape, lambda b: (0, 0, 0)))
    return pl.pallas_call(
        _swa_sample_body,
        out_shape=jax.ShapeDtypeStruct((nreq * SAMPLE_ROWS, SWA_INNER), bf16),
        grid=(nreq,),
        in_specs=in_specs,
        out_specs=pl.BlockSpec((SAMPLE_ROWS, SWA_INNER), lambda b: (b, 0)),
        compiler_params=_cparams("parallel"),
        name=name,
    )(p, *views, tb, bz)


def _t5_bucket(dist):
    max_exact = N_BUCKETS // 2
    dd = jnp.maximum(dist, 1).astype(f32)
    large = max_exact + (jnp.log(dd / max_exact) / math.log(MAX_DISTANCE / max_exact)
                         * (N_BUCKETS - max_exact)).astype(jnp.int32)
    return jnp.where(dist < max_exact, dist, jnp.minimum(large, N_BUCKETS - 1))


def _bias_tables(rel_bias, ntok):
    neg = -jnp.inf
    i = jnp.arange(SWA_KEYS)[:, None]
    jj = jnp.arange(2 * SWA_KEYS)[None, :]
    steps = SWA_KEYS + i - jj
    band = (steps >= 0) & (steps <= SWA_KEYS)
    m = jnp.arange(SWA_KEYS)
    prompt_t, sample_t, first_t = [], [], []
    for g, d in enumerate(SWA_DILATIONS):
        dist = jnp.arange(SWA_KEYS + 1, dtype=jnp.int32) * d
        bias_g = rel_bias[_t5_bucket(dist), g * SWA_HEADS:(g + 1) * SWA_HEADS].T.astype(f32)
        prompt_t.append(jnp.where(band[None], bias_g[:, jnp.clip(steps, 0, SWA_KEYS)], neg))
        per_tok = []
        for t in range(ntok):
            j = SWA_KEYS - m + (t if d == 1 else 0)
            ok = (j >= 1) & (j <= SWA_KEYS)
            per_tok.append(jnp.where(ok[None], bias_g[:, jnp.clip(j, 0, SWA_KEYS)], neg))
        sample_t.append(jnp.stack(per_tok))
        first_t.append(bias_g[:, :SWA_KEYS])
    return jnp.stack(prompt_t), jnp.stack(sample_t), jnp.stack(first_t)


def kernel(x_prompt, x_sample, state_gdn, state_conv, cache_win0, cache_win1, cache_win2, w_in_gdn, conv_w_gdn,
           a_log_gdn, dt_bias_gdn, norm_gdn, w_out_gdn, w_in_swa, w_out_swa, rel_bias, w_ffn_in, w_ffn_out,
           ln_mix_g, ln_mix_b, ln_ffn_g, ln_ffn_b):
    batch, seq, _ = x_prompt.shape
    nreq, ntok, _ = x_sample.shape
    caches = (cache_win0, cache_win1, cache_win2)
    mp = batch * seq
    tm_p = 1024
    tm_s = nreq * SAMPLE_ROWS

    xp = x_prompt.reshape(mp, D_MODEL)
    xs = jnp.pad(x_sample, ((0, 0), (0, SAMPLE_ROWS - ntok), (0, 0))).reshape(tm_s, D_MODEL)
    bias_p, bias_s, bias_first = _bias_tables(rel_bias, ntok)

    gdn_p, gdn_s, conv_p, conv_s = [], [], [], []
    win_p = [[] for _ in range(SWA_GROUPS)]
    win_s = [[] for _ in range(SWA_GROUPS)]
    for i in range(DEPTH):
        j = i // N_MIXERS
        ln_g, ln_b = ln_mix_g[i][None], ln_mix_b[i][None]
        if i % N_MIXERS == 0:
            w = w_in_gdn[j]
            w_qkvz = w[:, :GDN_QKVZ_DIM].astype(bf16)
            w_a = w[:, GDN_QKVZ_DIM:GDN_QKVZ_DIM + GDN_V_HEADS]
            w_b = w[:, GDN_QKVZ_DIM + GDN_V_HEADS:]
            fill = jnp.zeros((D_MODEL, LANES - 3 * GDN_V_HEADS), f32)
            w_ab = jnp.concatenate([w_a, w_b, w_a, fill], axis=1).astype(bf16)
            zpad = jnp.zeros((LANES - 3 * GDN_V_HEADS,), f32)
            zh = jnp.zeros((GDN_V_HEADS,), f32)
            alog = jnp.concatenate([a_log_gdn[j], zh, a_log_gdn[j], zpad])[None]
            dtb = jnp.concatenate([dt_bias_gdn[j], zh, dt_bias_gdn[j], zpad])[None]
            conv_w = conv_w_gdn[j]
            norm_w = norm_gdn[j][None]
            w_out = w_out_gdn[j].astype(bf16)

            pp = _proj(xp, w_qkvz, tm_p, 1024, f"gdn_in_p{j}")
            ps = _proj(xs, w_qkvz, tm_s, 1024, f"gdn_in_s{j}")
            gb_p, gbt_p = _gates(xp, w_ab, alog, dtb, tm_p, f"gdn_gates_p{j}")
            gb_s, _ = _gates(xs, w_ab, alog, dtb, tm_s, f"gdn_gates_s{j}")
            o_p, st_p = _gdn_prompt(pp, gb_p, gbt_p, conv_w, norm_w, batch, seq, f"gdn_prompt{j}")
            cbuf = jnp.pad(state_conv[j], ((0, 0), (0, SAMPLE_ROWS - (GDN_CONV_W - 1)), (0, 0)))
            o_s, st_s = _gdn_sample(ps, cbuf.reshape(tm_s, GDN_CONV_DIM), gb_s, conv_w, norm_w, state_gdn, j, 8,
                                    f"gdn_sample{j}")
            xp = _mm_ln(o_p, w_out, xp, ln_g, ln_b, tm_p, f"gdn_out_p{j}")
            xs = _mm_ln(o_s, w_out, xs, ln_g, ln_b, tm_s, f"gdn_out_s{j}")
            gdn_p.append(st_p)
            gdn_s.append(st_s)
            conv_p.append(pp.reshape(batch, seq, GDN_QKVZ_DIM)[:, seq - (GDN_CONV_W - 1):, :GDN_CONV_DIM])
            conv_s.append(ps.reshape(nreq, SAMPLE_ROWS, GDN_QKVZ_DIM)[:, ntok - (GDN_CONV_W - 1):ntok, :GDN_CONV_DIM])
        else:
            w_in = w_in_swa[j].astype(bf16)
            w_out = w_out_swa[j].astype(bf16)
            pp = _proj(xp, w_in, tm_p, 1024, f"swa_in_p{j}")
            ps = _proj(xs, w_in, tm_s, 1024, f"swa_in_s{j}")
            a_p = _swa_prompt(pp, bias_p, batch, seq, f"swa_prompt{j}")
            a_s = _swa_sample(ps, caches, j, bias_s, bias_first, f"swa_sample{j}")
            xp = _mm_ln(a_p, w_out, xp, ln_g, ln_b, tm_p, f"swa_out_p{j}")
            xs = _mm_ln(a_s, w_out, xs, ln_g, ln_b, tm_s, f"swa_out_s{j}")
            pp3 = pp.reshape(batch, seq, SWA_IN_DIM)
            ps3 = ps.reshape(nreq, SAMPLE_ROWS, SWA_IN_DIM)
            for g in range(SWA_GROUPS):
                keep = min(SWA_KEYS * SWA_DILATIONS[g], seq)
                lo, hi = (3 * g + 1) * SWA_INNER, (3 * g + 3) * SWA_INNER
                win_p[g].append(pp3[:, seq - keep:, lo:hi].reshape(batch, keep, 2, SWA_HEADS, SWA_HEAD_DIM))
                win_s[g].append(ps3[:, :ntok, lo:hi].reshape(nreq, ntok, 2, SWA_HEADS, SWA_HEAD_DIM))
        w_fi = w_ffn_in[i].astype(bf16)
        w_fo = w_ffn_out[i].astype(bf16)
        xp = _ffn(xp, w_fi, w_fo, ln_ffn_g[i][None], ln_ffn_b[i][None], tm_p, f"ffn_p{i}")
        xs = _ffn(xs, w_fi, w_fo, ln_ffn_g[i][None], ln_ffn_b[i][None], tm_s, f"ffn_s{i}")

    y_p = xp.reshape(batch, seq, D_MODEL)
    y_s = xs.reshape(nreq, SAMPLE_ROWS, D_MODEL)[:, :ntok]
    return (y_p, y_s, jnp.stack(gdn_p), jnp.stack(conv_p),
            jnp.stack(win_p[0]), jnp.stack(win_p[1]), jnp.stack(win_p[2]),
            jnp.stack(gdn_s), jnp.stack(conv_s),
            jnp.stack(win_s[0]), jnp.stack(win_s[1]), jnp.stack(win_s[2]))
```

```python
import functools
import math

import jax
import jax.numpy as jnp
from jax import lax
from jax.experimental import pallas as pl
from jax.experimental.pallas import tpu as pltpu

f32 = jnp.float32
bf16 = jnp.bfloat16

D_MODEL = 1024
DEPTH = 4
N_MIXERS = 2
GDN_QK_HEADS = 8
GDN_V_HEADS = 16
GDN_DK = 128
GDN_DV = 128
GDN_CONV_W = 4
GDN_QK_DIM = GDN_QK_HEADS * GDN_DK
GDN_V_DIM = GDN_V_HEADS * GDN_DV
GDN_CONV_DIM = 2 * GDN_QK_DIM + GDN_V_DIM
GDN_QKVZ_DIM = GDN_CONV_DIM + GDN_V_DIM
SWA_DILATIONS = (1, 4, 16)
SWA_GROUPS = 3
SWA_HEADS = 16
SWA_HEAD_DIM = 64
SWA_KEYS = 128
SWA_INNER = SWA_HEADS * SWA_HEAD_DIM
SWA_IN_DIM = SWA_GROUPS * 3 * SWA_INNER
N_BUCKETS = 32
MAX_DISTANCE = 2048
D_FF = 2816
DEEPNORM_ALPHA = (2 * DEPTH) ** 0.25
LN_EPS = 1e-5
RMS_EPS = 1e-6
L2_EPS = 1e-6

LANES = 128
SUBLANES = 8
VMEM_LIMIT = 56 * 1024 * 1024

CHUNK = 128
PREP_CHUNKS = 8
OUT_CHUNKS = 4
ATT_BLOCKS = 4
SAMPLE_ROWS = 8
FFN_TILE = 256


def _cparams(*sem):
    return pltpu.CompilerParams(dimension_semantics=sem, vmem_limit_bytes=VMEM_LIMIT)


def _dot(a, b):
    return jnp.dot(a, b, preferred_element_type=f32)


def _dot_nt(a, b):
    return lax.dot_general(a, b, (((1,), (1,)), ((), ())), preferred_element_type=f32)


def _sigmoid(x):
    return 1.0 / (1.0 + jnp.exp(-x))


def _silu(x):
    return x * _sigmoid(x)


def _layer_norm(y, g, b):
    mu = jnp.mean(y, axis=-1, keepdims=True)
    yc = y - mu
    var = jnp.mean(yc * yc, axis=-1, keepdims=True)
    return yc * lax.rsqrt(var + LN_EPS) * g + b


def _proj_body(x_ref, w_ref, o_ref, xb_ref):
    @pl.when(pl.program_id(1) == 0)
    def _():
        xb_ref[...] = x_ref[...].astype(bf16)

    o_ref[...] = _dot(xb_ref[...], w_ref[...]).astype(o_ref.dtype)


def _proj(x, w, tm, tn, name):
    m, k = x.shape
    n = w.shape[1]
    return pl.pallas_call(
        _proj_body,
        out_shape=jax.ShapeDtypeStruct((m, n), f32),
        grid=(m // tm, n // tn),
        in_specs=[pl.BlockSpec((tm, k), lambda i, j: (i, 0)),
                  pl.BlockSpec((k, tn), lambda i, j: (0, j))],
        out_specs=pl.BlockSpec((tm, tn), lambda i, j: (i, j)),
        scratch_shapes=[pltpu.VMEM((tm, k), bf16)],
        compiler_params=_cparams("parallel", "arbitrary"),
        name=name,
    )(x, w)


def _mm_ln_body(a_ref, w_ref, x_ref, g_ref, b_ref, o_ref):
    h = _dot(a_ref[...].astype(bf16), w_ref[...])
    o_ref[...] = _layer_norm(DEEPNORM_ALPHA * x_ref[...] + h, g_ref[...], b_ref[...])


def _mm_ln(a, w, x, g, b, tm, name):
    m, k = a.shape
    return pl.pallas_call(
        _mm_ln_body,
        out_shape=jax.ShapeDtypeStruct((m, D_MODEL), f32),
        grid=(m // tm,),
        in_specs=[pl.BlockSpec((tm, k), lambda i: (i, 0)),
                  pl.BlockSpec((k, D_MODEL), lambda i: (0, 0)),
                  pl.BlockSpec((tm, D_MODEL), lambda i: (i, 0)),
                  pl.BlockSpec((1, D_MODEL), lambda i: (0, 0)),
                  pl.BlockSpec((1, D_MODEL), lambda i: (0, 0))],
        out_specs=pl.BlockSpec((tm, D_MODEL), lambda i: (i, 0)),
        compiler_params=_cparams("parallel"),
        name=name,
    )(a, w, x, g, b)


def _ffn_body(x_ref, w1_ref, w2_ref, wo_ref, g_ref, b_ref, o_ref, xb_ref, acc_ref):
    j = pl.program_id(1)

    @pl.when(j == 0)
    def _():
        xb_ref[...] = x_ref[...].astype(bf16)
        acc_ref[...] = jnp.zeros_like(acc_ref)

    xb = xb_ref[...]
    act = _silu(_dot(xb, w1_ref[...])) * _dot(xb, w2_ref[...])
    acc_ref[...] += _dot(act.astype(bf16), wo_ref[...])

    @pl.when(j == pl.num_programs(1) - 1)
    def _():
        o_ref[...] = _layer_norm(DEEPNORM_ALPHA * x_ref[...] + acc_ref[...], g_ref[...], b_ref[...])


def _ffn(x, w_in, w_out, g, b, tm, name):
    m = x.shape[0]
    nf = D_FF // FFN_TILE
    return pl.pallas_call(
        _ffn_body,
        out_shape=jax.ShapeDtypeStruct((m, D_MODEL), f32),
        grid=(m // tm, nf),
        in_specs=[pl.BlockSpec((tm, D_MODEL), lambda i, j: (i, 0)),
                  pl.BlockSpec((D_MODEL, FFN_TILE), lambda i, j: (0, j)),
                  pl.BlockSpec((D_MODEL, FFN_TILE), lambda i, j: (0, j + nf)),
                  pl.BlockSpec((FFN_TILE, D_MODEL), lambda i, j: (j, 0)),
                  pl.BlockSpec((1, D_MODEL), lambda i, j: (0, 0)),
                  pl.BlockSpec((1, D_MODEL), lambda i, j: (0, 0))],
        out_specs=pl.BlockSpec((tm, D_MODEL), lambda i, j: (i, 0)),
        scratch_shapes=[pltpu.VMEM((tm, D_MODEL), bf16), pltpu.VMEM((tm, D_MODEL), f32)],
        compiler_params=_cparams("parallel", "arbitrary"),
        name=name,
    )(x, w_in, w_in, w_out, g, b)


def _gates_body(x_ref, w_ref, al_ref, dt_ref, gb_ref, gbt_ref):
    ab = _dot(x_ref[...].astype(bf16), w_ref[...])
    a = ab + dt_ref[...]
    softplus = jnp.maximum(a, 0.0) + jnp.log1p(jnp.exp(-jnp.abs(a)))
    g = -jnp.exp(al_ref[...]) * softplus
    beta = _sigmoid(ab)
    tm = ab.shape[0]
    row = lax.broadcasted_iota(jnp.int32, (CHUNK, CHUNK), 0)
    col = lax.broadcasted_iota(jnp.int32, (CHUNK, CHUNK), 1)
    tril = (row >= col).astype(f32)
    gc = jnp.concatenate(
        [jnp.dot(tril, g[c * CHUNK:(c + 1) * CHUNK], precision=lax.Precision.HIGHEST,
                 preferred_element_type=f32) for c in range(tm // CHUNK)], axis=0)
    lane = lax.broadcasted_iota(jnp.int32, (1, LANES), 1)
    gb = jnp.where(lane < GDN_V_HEADS, gc, jnp.where(lane < 2 * GDN_V_HEADS, beta, g))
    gb_ref[...] = gb
    gbt_ref[...] = gb.T


def _gates(x, w_ab, alog, dtb, tm, name):
    m = x.shape[0]
    return pl.pallas_call(
        _gates_body,
        out_shape=(jax.ShapeDtypeStruct((m, LANES), f32), jax.ShapeDtypeStruct((LANES, m), f32)),
        grid=(m // tm,),
        in_specs=[pl.BlockSpec((tm, D_MODEL), lambda i: (i, 0)),
                  pl.BlockSpec((D_MODEL, LANES), lambda i: (0, 0)),
                  pl.BlockSpec((1, LANES), lambda i: (0, 0)),
                  pl.BlockSpec((1, LANES), lambda i: (0, 0))],
        out_specs=(pl.BlockSpec((tm, LANES), lambda i: (i, 0)),
                   pl.BlockSpec((LANES, tm), lambda i: (0, i))),
        compiler_params=_cparams("parallel"),
        name=name,
    )(x, w_ab, alog, dtb)


def _lane_column(x, lane, idx):
    return jnp.sum(jnp.where(lane == idx, x, 0.0), axis=1, keepdims=True)


def _l2norm(x):
    return x * lax.rsqrt(jnp.sum(x * x, axis=-1, keepdims=True) + L2_EPS)


def _gated_rmsnorm(o, nw, z):
    return o * lax.rsqrt(jnp.mean(o * o, axis=-1, keepdims=True) + RMS_EPS) * nw * _silu(z)


def _unit_lower_inverses(neg_ls, eye, row, col):
    def same_block(size):
        return (row // size) == (col // size)

    base = 2 * SUBLANES
    inside = same_block(base)
    powers = [jnp.where(inside, n, 0.0) for n in neg_ls]
    invs = [eye + p for p in powers]
    span = 2
    while span < base:
        pbs = [p.astype(bf16) for p in powers]
        powers = [_dot(pb, pb) for pb in pbs]
        invs = [inv + _dot(inv.astype(bf16), p.astype(bf16)) for inv, p in zip(invs, powers)]
        span *= 2
    size = base
    while size < CHUNK:
        wider = same_block(2 * size)
        offs = [jnp.where(wider & ~inside, n, 0.0).astype(bf16) for n in neg_ls]
        inv_bs = [inv.astype(bf16) for inv in invs]
        mids = [_dot(off, inv_b).astype(bf16) for off, inv_b in zip(offs, inv_bs)]
        invs = [inv + _dot(inv_b, mid) for inv, inv_b, mid in zip(invs, inv_bs, mids)]
        inside = wider
        size *= 2
    return invs


def _gdn_prompt_body(q_ref, k_ref, v_ref, z_ref, gb_ref, gbt_ref, cwq_ref, cwk_ref, cwv_ref, nw_ref,
                     o_ref, st_ref, sb_scr, ob_scr, sk_scr, oq_scr, st_scr, el_scr, s_scr):
    hq = pl.program_id(1)
    nchunk = q_ref.shape[0] // CHUNK
    row = lax.broadcasted_iota(jnp.int32, (CHUNK, CHUNK), 0)
    col = lax.broadcasted_iota(jnp.int32, (CHUNK, CHUNK), 1)
    tri = row >= col
    strict = row > col
    eye = (row == col).astype(f32)
    lane = lax.broadcasted_iota(jnp.int32, (1, LANES), 1)
    rows8 = lax.broadcasted_iota(jnp.int32, (SUBLANES, 1), 0)
    cwq = cwq_ref[...]
    cwk = cwk_ref[...]
    cwv = cwv_ref[...]
    nw = nw_ref[...]

    tile0 = pl.multiple_of((2 * hq) // SUBLANES * SUBLANES, SUBLANES)

    def conv_silu(ref, c, r0, cw):
        x = ref[pl.ds(r0, CHUNK), :]
        p0 = pl.multiple_of(jnp.maximum(r0 - SUBLANES, 0), SUBLANES)
        prev8 = jnp.where(c > 0, ref[pl.ds(p0, SUBLANES), :], 0.0)
        acc = x * cw[GDN_CONV_W - 1:GDN_CONV_W]
        for s in range(1, GDN_CONV_W):
            rolled = pltpu.roll(x, s, 0)
            top = jnp.where(rows8 < s, pltpu.roll(prev8, s, 0), rolled[:SUBLANES])
            shifted = jnp.concatenate([top, rolled[SUBLANES:]], axis=0)
            acc = acc + shifted * cw[GDN_CONV_W - 1 - s:GDN_CONV_W - s]
        return _silu(acc)

    def prepare(i, carry):
        chains = []
        for cc in range(PREP_CHUNKS):
            c = i * PREP_CHUNKS + cc
            r0 = pl.multiple_of(c * CHUNK, CHUNK)
            qn = _l2norm(conv_silu(q_ref, c, r0, cwq)) * GDN_DK ** -0.5
            kn = _l2norm(conv_silu(k_ref, c, r0, cwk))
            vc = conv_silu(v_ref, c, r0, cwv)
            kb = kn.astype(bf16)
            kk = _dot_nt(kb, kb)
            qk = _dot_nt(qn.astype(bf16), kb)
            k_t = kn.T
            gbc = gb_ref[pl.ds(r0, CHUNK), :]
            gbt8 = gbt_ref[pl.ds(tile0, SUBLANES), pl.ds(r0, CHUNK)]
            for hh in range(2):
                hv = 2 * hq + hh
                gc_col = _lane_column(gbc, lane, hv)
                beta_col = _lane_column(gbc, lane, GDN_V_HEADS + hv)
                gc_row = jnp.sum(jnp.where(rows8 == hv % SUBLANES, gbt8, 0.0), axis=0, keepdims=True)
                decay = jnp.where(tri, jnp.exp(jnp.where(tri, gc_col - gc_row, 0.0)), 0.0)
                g_last = gc_row[:, CHUNK - 1:CHUNK]
                el_scr[hh, c] = jnp.broadcast_to(jnp.exp(g_last), (SUBLANES, GDN_DV))
                vh = vc[:, hh * GDN_DV:(hh + 1) * GDN_DV]
                chains.append(dict(
                    hh=hh, c=c,
                    neg_l=jnp.where(strict, -(beta_col * kk * decay), 0.0),
                    vb=(vh * beta_col).astype(bf16),
                    kbg=(kn * (beta_col * jnp.exp(gc_col))).astype(bf16),
                    a=jnp.where(tri, qk * decay, 0.0).astype(bf16),
                    qg=qn * jnp.exp(gc_col),
                    kd=(k_t * jnp.exp(g_last - gc_row)).astype(bf16)))
        invs = _unit_lower_inverses([ch["neg_l"] for ch in chains], eye, row, col)
        inv_bs = [inv.astype(bf16) for inv in invs]
        us = [_dot(inv_b, ch["vb"]).astype(bf16) for ch, inv_b in zip(chains, inv_bs)]
        ws = [_dot(inv_b, ch["kbg"]).astype(bf16) for ch, inv_b in zip(chains, inv_bs)]
        for ch, u, w in zip(chains, us, ws):
            hh, c = ch["hh"], ch["c"]
            sk_scr[hh, c] = _dot(ch["kd"], w).astype(bf16)
            sb_scr[hh, c] = _dot(ch["kd"], u)
            oq_scr[hh, c] = (ch["qg"] - _dot(ch["a"], w)).astype(bf16)
            ob_scr[hh, c] = _dot(ch["a"], u)
        return carry

    def advance(c, carry):
        for hh in range(2):
            state = s_scr[hh]
            state_b = state.astype(bf16)
            st_scr[hh, c] = state_b
            s_scr[hh] = state * el_scr[hh, c][0:1] + (sb_scr[hh, c] - _dot(sk_scr[hh, c], state_b))
        return carry

    def emit(i, carry):
        os_ = []
        for cc in range(OUT_CHUNKS):
            c = i * OUT_CHUNKS + cc
            for hh in range(2):
                os_.append((c, hh, _dot(oq_scr[hh, c], st_scr[hh, c]) + ob_scr[hh, c]))
        for cc in range(OUT_CHUNKS):
            c = i * OUT_CHUNKS + cc
            r0 = pl.multiple_of(c * CHUNK, CHUNK)
            zc = z_ref[pl.ds(r0, CHUNK), :]
            outs = [_gated_rmsnorm(o, nw, zc[:, hh * GDN_DV:(hh + 1) * GDN_DV])
                    for (c2, hh, o) in os_[2 * cc:2 * cc + 2]]
            o_ref[pl.ds(r0, CHUNK), :] = jnp.concatenate(outs, axis=1).astype(o_ref.dtype)
        return carry

    lax.fori_loop(0, nchunk // PREP_CHUNKS, prepare, 0)
    s_scr[...] = jnp.zeros_like(s_scr)
    lax.fori_loop(0, nchunk, advance, 0)
    st_ref[0] = s_scr[...]
    lax.fori_loop(0, nchunk // OUT_CHUNKS, emit, 0)


def _gdn_prompt(p, gb, gbt, conv_w, norm_w, batch, seq, name):
    nqk = GDN_QK_HEADS
    zoff = GDN_CONV_DIM // (2 * GDN_DV)
    nchunk = seq // CHUNK
    tiles_bf = pltpu.VMEM((2, nchunk, CHUNK, CHUNK), bf16)
    tiles_f32 = pltpu.VMEM((2, nchunk, CHUNK, CHUNK), f32)
    return pl.pallas_call(
        _gdn_prompt_body,
        out_shape=(jax.ShapeDtypeStruct((batch * seq, GDN_V_DIM), bf16),
                   jax.ShapeDtypeStruct((batch, GDN_V_HEADS, GDN_DK, GDN_DV), f32)),
        grid=(batch, nqk),
        in_specs=[pl.BlockSpec((seq, GDN_DK), lambda b, h: (b, h)),
                  pl.BlockSpec((seq, GDN_DK), lambda b, h: (b, nqk + h)),
                  pl.BlockSpec((seq, 2 * GDN_DV), lambda b, h: (b, nqk + h)),
                  pl.BlockSpec((seq, 2 * GDN_DV), lambda b, h: (b, zoff + h)),
                  pl.BlockSpec((seq, LANES), lambda b, h: (b, 0)),
                  pl.BlockSpec((LANES, seq), lambda b, h: (0, b)),
                  pl.BlockSpec((GDN_CONV_W, GDN_DK), lambda b, h: (0, h)),
                  pl.BlockSpec((GDN_CONV_W, GDN_DK), lambda b, h: (0, nqk + h)),
                  pl.BlockSpec((GDN_CONV_W, 2 * GDN_DV), lambda b, h: (0, nqk + h)),
                  pl.BlockSpec((1, GDN_DV), lambda b, h: (0, 0))],
        out_specs=(pl.BlockSpec((seq, 2 * GDN_DV), lambda b, h: (b, h)),
                   pl.BlockSpec((1, 2, GDN_DK, GDN_DV), lambda b, h: (b, h, 0, 0))),
        scratch_shapes=[tiles_f32, tiles_f32, tiles_bf, tiles_bf, tiles_bf,
                        pltpu.VMEM((2, nchunk, SUBLANES, GDN_DV), f32),
                        pltpu.VMEM((2, GDN_DK, GDN_DV), f32)],
        compiler_params=_cparams("parallel", "parallel"),
        name=name,
    )(p, p, p, p, gb, gbt, conv_w, conv_w, conv_w, norm_w)


def _gdn_sample_body(q_ref, k_ref, v_ref, z_ref, aq_ref, ak_ref, av_ref, gb_ref, cwq_ref, cwk_ref, cwv_ref,
                     nw_ref, st_ref, o_ref, sto_ref, q_scr, k_scr, v_scr, o_scr):
    hq = pl.program_id(1)
    rows = q_ref.shape[0]
    nreq = rows // SAMPLE_ROWS
    tok = lax.broadcasted_iota(jnp.int32, (rows, 1), 0) % SAMPLE_ROWS
    lane = lax.broadcasted_iota(jnp.int32, (1, LANES), 1)
    rows8 = lax.broadcasted_iota(jnp.int32, (SAMPLE_ROWS, 1), 0)
    nw = nw_ref[...]
    hist = GDN_CONV_W - 1

    def conv_silu(a, x, cw):
        acc = jnp.zeros_like(x)
        for w in range(GDN_CONV_W):
            ra = a if w == 0 else pltpu.roll(a, rows - w, 0)
            rx = x if w == hist else pltpu.roll(x, hist - w, 0)
            acc = acc + jnp.where(tok + w < hist, ra, rx) * cw[w:w + 1]
        return _silu(acc)

    q_scr[...] = _l2norm(conv_silu(aq_ref[...], q_ref[...], cwq_ref[...])) * GDN_DK ** -0.5
    k_scr[...] = _l2norm(conv_silu(ak_ref[...], k_ref[...], cwk_ref[...]))
    v_scr[...] = conv_silu(av_ref[...], v_ref[...], cwv_ref[...])
    pad = jnp.zeros((LANES - 2 * SAMPLE_ROWS, GDN_DK), f32)

    def request(bi, carry):
        r0 = pl.multiple_of(bi * SAMPLE_ROWS, SAMPLE_ROWS)
        k8 = k_scr[pl.ds(r0, SAMPLE_ROWS), :]
        q8 = q_scr[pl.ds(r0, SAMPLE_ROWS), :]
        v8 = v_scr[pl.ds(r0, SAMPLE_ROWS), :]
        z8 = z_ref[pl.ds(r0, SAMPLE_ROWS), :]
        gb8 = gb_ref[pl.ds(r0, SAMPLE_ROWS), :]
        kq_t = jnp.concatenate([k8, q8, pad], axis=0).T
        outs = []
        for hh in range(2):
            hv = 2 * hq + hh
            g_col = _lane_column(gb8, lane, 2 * GDN_V_HEADS + hv)
            beta_col = _lane_column(gb8, lane, GDN_V_HEADS + hv)
            state = st_ref[bi, hh]
            o_acc = jnp.zeros((SAMPLE_ROWS, GDN_DV), f32)
            for t in range(GDN_CONV_W):
                k_col = kq_t[:, t:t + 1]
                q_col = kq_t[:, SAMPLE_ROWS + t:SAMPLE_ROWS + t + 1]
                state = state * jnp.exp(g_col[t:t + 1])
                ks = jnp.sum(state * k_col, axis=0, keepdims=True)
                delta = (v8[t:t + 1, hh * GDN_DV:(hh + 1) * GDN_DV] - ks) * beta_col[t:t + 1]
                state = state + k_col * delta
                o_t = jnp.sum(state * q_col, axis=0, keepdims=True)
                o_acc = jnp.where(rows8 == t, o_t, o_acc)
            sto_ref[bi, hh] = state
            outs.append(_gated_rmsnorm(o_acc, nw, z8[:, hh * GDN_DV:(hh + 1) * GDN_DV]))
        o_scr[pl.ds(r0, SAMPLE_ROWS), :] = jnp.concatenate(outs, axis=1)
        return carry

    lax.fori_loop(0, nreq, request, 0)
    o_ref[...] = o_scr[...].astype(o_ref.dtype)


def _gdn_sample(p, conv_buf, gb, conv_w, norm_w, state, layer, nreq_blk, name):
    assert GDN_CONV_W == 4 and SAMPLE_ROWS == 8
    nqk = GDN_QK_HEADS
    zoff = GDN_CONV_DIM // (2 * GDN_DV)
    nreq = state.shape[1]
    rows = nreq_blk * SAMPLE_ROWS
    st_spec = pl.BlockSpec((None, nreq_blk, 2, GDN_DK, GDN_DV), lambda i, h: (layer, i, h, 0, 0))
    return pl.pallas_call(
        _gdn_sample_body,
        out_shape=(jax.ShapeDtypeStruct((nreq * SAMPLE_ROWS, GDN_V_DIM), bf16),
                   jax.ShapeDtypeStruct((nreq, GDN_V_HEADS, GDN_DK, GDN_DV), f32)),
        grid=(nreq // nreq_blk, nqk),
        in_specs=[pl.BlockSpec((rows, GDN_DK), lambda i, h: (i, h)),
                  pl.BlockSpec((rows, GDN_DK), lambda i, h: (i, nqk + h)),
                  pl.BlockSpec((rows, 2 * GDN_DV), lambda i, h: (i, nqk + h)),
                  pl.BlockSpec((rows, 2 * GDN_DV), lambda i, h: (i, zoff + h)),
                  pl.BlockSpec((rows, GDN_DK), lambda i, h: (i, h)),
                  pl.BlockSpec((rows, GDN_DK), lambda i, h: (i, nqk + h)),
                  pl.BlockSpec((rows, 2 * GDN_DV), lambda i, h: (i, nqk + h)),
                  pl.BlockSpec((rows, LANES), lambda i, h: (i, 0)),
                  pl.BlockSpec((GDN_CONV_W, GDN_DK), lambda i, h: (0, h)),
                  pl.BlockSpec((GDN_CONV_W, GDN_DK), lambda i, h: (0, nqk + h)),
                  pl.BlockSpec((GDN_CONV_W, 2 * GDN_DV), lambda i, h: (0, nqk + h)),
                  pl.BlockSpec((1, GDN_DV), lambda i, h: (0, 0)),
                  st_spec],
        out_specs=(pl.BlockSpec((rows, 2 * GDN_DV), lambda i, h: (i, h)),
                   pl.BlockSpec((nreq_blk, 2, GDN_DK, GDN_DV), lambda i, h: (i, h, 0, 0))),
        scratch_shapes=[pltpu.VMEM((rows, GDN_DK), f32), pltpu.VMEM((rows, GDN_DK), f32),
                        pltpu.VMEM((rows, 2 * GDN_DV), f32), pltpu.VMEM((rows, 2 * GDN_DV), f32)],
        compiler_params=_cparams("parallel", "parallel"),
        name=name,
    )(p, p, p, p, conv_buf, conv_buf, conv_buf, gb, conv_w, conv_w, conv_w, norm_w, state)


def _swa_prompt_body(*refs):
    qkv_refs = refs[:3 * SWA_GROUPS]
    bias_ref, o_ref, o_scr, l_scr = refs[3 * SWA_GROUPS:]
    seq = o_ref.shape[0]
    lane = lax.broadcasted_iota(jnp.int32, (1, LANES), 1)
    head0 = lane < SWA_HEAD_DIM
    blk = SWA_KEYS
    for g, d in enumerate(SWA_DILATIONS):
        q_ref, k_ref, v_ref = qkv_refs[3 * g:3 * g + 3]
        nb = seq // (d * blk)

        def rows_of(start, d=d):
            return pl.ds(start, blk) if d == 1 else pl.ds(start, blk, stride=d)

        def blocks(i, carry, g=g, d=d, nb=nb, q_ref=q_ref, k_ref=k_ref, v_ref=v_ref, rows_of=rows_of):
            has_prev = nb > 1
            work = []
            for bb in range(ATT_BLOCKS):
                idx = i * ATT_BLOCKS + bb
                r = idx // nb
                b = idx % nb
                cur = rows_of(r + d * blk * b)
                q = q_ref[cur, :] * SWA_HEAD_DIM ** -0.5
                kc = k_ref[cur, :].astype(bf16)
                vc = v_ref[cur, :].astype(bf16)
                kp = vp = no_prev = None
                if has_prev:
                    prev = rows_of(r + d * blk * jnp.maximum(b - 1, 0))
                    no_prev = jnp.where(b == 0, -jnp.inf, 0.0).astype(f32)
                    kp = k_ref[prev, :].astype(bf16)
                    vp = v_ref[prev, :].astype(bf16)
                for hh in range(2):
                    qh = jnp.where(head0 if hh == 0 else ~head0, q, 0.0).astype(bf16)
                    work.append(dict(cur=cur, hh=hh, qh=qh, kc=kc, vc=vc, kp=kp, vp=vp, no_prev=no_prev))
            for w in work:
                w["s_c"] = _dot_nt(w["qh"], w["kc"]) + bias_ref[g, w["hh"]][:, blk:]
            if has_prev:
                for w in work:
                    w["s_p"] = _dot_nt(w["qh"], w["kp"]) + (bias_ref[g, w["hh"]][:, :blk] + w["no_prev"])
            for w in work:
                m = jnp.max(w["s_c"], axis=-1, keepdims=True)
                if has_prev:
                    m = jnp.maximum(m, jnp.max(w["s_p"], axis=-1, keepdims=True))
                p_c = jnp.exp(w["s_c"] - m)
                den = jnp.sum(p_c, axis=-1, keepdims=True)
                w["p_c"] = p_c.astype(bf16)
                if has_prev:
                    p_p = jnp.exp(w["s_p"] - m)
                    den = den + jnp.sum(p_p, axis=-1, keepdims=True)
                    w["p_p"] = p_p.astype(bf16)
                w["m"], w["den"] = m, den
            for w in work:
                w["pv"] = _dot(w["p_c"], w["vc"])
            if has_prev:
                for w in work:
                    w["pv"] = w["pv"] + _dot(w["p_p"], w["vp"])
            for w0, w1 in zip(work[0::2], work[1::2]):
                o_scr[g, w0["cur"], :] = jnp.where(head0, w0["pv"] / w0["den"], w1["pv"] / w1["den"])
                l_scr[g, w0["cur"], :] = jnp.where(head0, w0["m"] + jnp.log(w0["den"]), w1["m"] + jnp.log(w1["den"]))
            return carry

        lax.fori_loop(0, seq // (blk * ATT_BLOCKS), blocks, 0)

    def merge(c, carry):
        rr = pl.ds(pl.multiple_of(c * blk, blk), blk)
        ls = [l_scr[g, rr, :] for g in range(SWA_GROUPS)]
        mx = jnp.maximum(jnp.maximum(ls[0], ls[1]), ls[2])
        ws = [jnp.exp(l - mx) for l in ls]
        num = ws[0] * o_scr[0, rr, :] + ws[1] * o_scr[1, rr, :] + ws[2] * o_scr[2, rr, :]
        o_ref[rr, :] = (num / (ws[0] + ws[1] + ws[2])).astype(o_ref.dtype)
        return carry

    lax.fori_loop(0, seq // blk, merge, 0)


def _swa_prompt(p, bias_t, batch, seq, name):
    npair = SWA_INNER // LANES
    in_specs = []
    for g in range(SWA_GROUPS):
        for c in range(3):
            off = (g * 3 + c) * npair
            in_specs.append(pl.BlockSpec((seq, LANES), lambda b, h, off=off: (b, off + h)))
    in_specs.append(pl.BlockSpec((SWA_GROUPS, 2, SWA_KEYS, 2 * SWA_KEYS), lambda b, h: (0, h, 0, 0)))
    return pl.pallas_call(
        _swa_prompt_body,
        out_shape=jax.ShapeDtypeStruct((batch * seq, SWA_INNER), bf16),
        grid=(batch, npair),
        in_specs=in_specs,
        out_specs=pl.BlockSpec((seq, LANES), lambda b, h: (b, h)),
        scratch_shapes=[pltpu.VMEM((SWA_GROUPS, seq, LANES), f32), pltpu.VMEM((SWA_GROUPS, seq, LANES), f32)],
        compiler_params=_cparams("parallel", "parallel"),
        name=name,
    )(*([p] * (3 * SWA_GROUPS)), bias_t)


def _swa_sample_body(q_ref, c0_ref, c1_ref, c2_ref, tb_ref, bz_ref, o_ref):
    ntok = q_ref.shape[0]
    nkeys = SWA_KEYS * SWA_HEADS
    cache_refs = (c0_ref, c1_ref, c2_ref)

    def flat_bf16(x):
        return x.reshape(nkeys, SWA_HEAD_DIM).astype(bf16)

    work = []
    for t in range(ntok):
        for g, d in enumerate(SWA_DILATIONS):
            work.append(dict(t=t, g=g, d=d, q=q_ref[t, g, 0] * SWA_HEAD_DIM ** -0.5))
    k_shared = flat_bf16(c0_ref[:, 0])
    for w in work:
        k_buf = k_shared if w["d"] == 1 else flat_bf16(cache_refs[w["g"]][:, w["t"], 0])
        w["s"] = _dot_nt(w["q"].astype(bf16), k_buf) + tb_ref[w["g"], w["t"]]
    for w in work:
        t, g, d = w["t"], w["g"], w["d"]
        m = jnp.max(w["s"], axis=-1, keepdims=True)
        new = []
        for tn in (range(t + 1) if d == 1 else (t,)):
            j = (t - tn) // d
            s_n = jnp.sum(w["q"] * q_ref[tn, g, 1], axis=-1, keepdims=True) + bz_ref[g][:, j:j + 1]
            m = jnp.maximum(m, s_n)
            new.append((s_n, tn))
        p_buf = jnp.exp(w["s"] - m)
        w["den"] = jnp.sum(p_buf, axis=-1, keepdims=True)
        w["p"] = p_buf.astype(bf16)
        w["m"], w["new"] = m, new
    v_shared = flat_bf16(c0_ref[:, 1])
    for w in work:
        v_buf = v_shared if w["d"] == 1 else flat_bf16(cache_refs[w["g"]][:, w["t"], 1])
        w["pv"] = _dot(w["p"], v_buf)
    for t in range(ntok):
        o_g, l_g = [], []
        for w in work[t * SWA_GROUPS:(t + 1) * SWA_GROUPS]:
            den, pv = w["den"], w["pv"]
            for s_n, tn in w["new"]:
                p_n = jnp.exp(s_n - w["m"])
                den = den + p_n
                pv = pv + p_n * q_ref[tn, w["g"], 2]
            o_g.append(pv / den)
            l_g.append(w["m"] + jnp.log(den))
        mx = jnp.maximum(jnp.maximum(l_g[0], l_g[1]), l_g[2])
        ws = [jnp.exp(l - mx) for l in l_g]
        o_ref[t] = (ws[0] * o_g[0] + ws[1] * o_g[1] + ws[2] * o_g[2]) / (ws[0] + ws[1] + ws[2])


def _swa_sample(qkv, caches, layer, tb, bz, name):
    nreq, ntok = qkv.shape[:2]
    in_specs = [pl.BlockSpec((None,) + qkv.shape[1:], lambda b: (b, 0, 0, 0, 0, 0))]
    views = []
    for c, d in zip(caches, SWA_DILATIONS):
        nl, nb, length = c.shape[:3]
        assert nb == nreq and length == SWA_KEYS * d and (d == 1 or d % ntok == 0)
        if d == 1:
            views.append(c)
            in_specs.append(pl.BlockSpec((None, None, SWA_KEYS, 2, SWA_HEADS, SWA_HEAD_DIM),
                                         lambda b: (layer, b, 0, 0, 0, 0)))
        else:
            views.append(c.reshape(nl, nb, SWA_KEYS, d, 2, SWA_HEADS, SWA_HEAD_DIM))
            in_specs.append(pl.BlockSpec((None, None, SWA_KEYS, ntok, 2, SWA_HEADS, SWA_HEAD_DIM),
                                         lambda b: (layer, b, 0, 0, 0, 0, 0)))
    in_specs.append(pl.BlockSpec(tb.shape, lambda b: (0, 0, 0, 0)))
    in_specs.append(pl.BlockSpec(bz.shape, lambda b: (0, 0, 0)))
    return pl.pallas_call(
        _swa_sample_body,
        out_shape=jax.ShapeDtypeStruct((nreq, ntok, SWA_HEADS, SWA_HEAD_DIM), f32),
        grid=(nreq,),
        in_specs=in_specs,
        out_specs=pl.BlockSpec((None, ntok, SWA_HEADS, SWA_HEAD_DIM), lambda b: (b, 0, 0, 0)),
        compiler_params=_cparams("parallel"),
        name=name,
    )(qkv, *views, tb, bz)


def _t5_bucket(dist):
    max_exact = N_BUCKETS // 2
    dd = jnp.maximum(dist, 1).astype(f32)
    large = max_exact + (jnp.log(dd / max_exact) / math.log(MAX_DISTANCE / max_exact)
                         * (N_BUCKETS - max_exact)).astype(jnp.int32)
    return jnp.where(dist < max_exact, dist, jnp.minimum(large, N_BUCKETS - 1))


def _select_rows(table, idx):
    onehot = (idx[..., None] == jnp.arange(table.shape[0])).astype(f32)
    return jnp.einsum("...n,nc->...c", onehot, table, precision=lax.Precision.HIGHEST)


def _bias_tables(rel_bias, ntok):
    neg = -jnp.inf
    i = jnp.arange(SWA_KEYS)[:, None]
    jj = jnp.arange(2 * SWA_KEYS)[None, :]
    steps = SWA_KEYS + i - jj
    band = (steps >= 0) & (steps <= SWA_KEYS)
    m = jnp.arange(SWA_KEYS)
    same_head = jnp.eye(SWA_HEADS, dtype=bool)[:, None, :]
    prompt_t, sample_t, first_t = [], [], []
    for g, d in enumerate(SWA_DILATIONS):
        dist = jnp.arange(SWA_KEYS + 1, dtype=jnp.int32) * d
        bias_g = _select_rows(rel_bias[:, g * SWA_HEADS:(g + 1) * SWA_HEADS], _t5_bucket(dist))
        toeplitz = _select_rows(bias_g, jnp.clip(steps, 0, SWA_KEYS))
        prompt_t.append(jnp.where(band[None], jnp.transpose(toeplitz, (2, 0, 1)), neg))
        per_tok = []
        for t in range(ntok):
            j = SWA_KEYS - m + (t if d == 1 else 0)
            ok = (j >= 1) & (j <= SWA_KEYS)
            b_hm = jnp.where(ok[None], _select_rows(bias_g, jnp.clip(j, 0, SWA_KEYS)).T, neg)
            per_tok.append(jnp.where(same_head, b_hm[:, :, None], neg).reshape(SWA_HEADS, SWA_KEYS * SWA_HEADS))
        sample_t.append(jnp.stack(per_tok))
        first_t.append(bias_g[:SWA_KEYS].T)
    return jnp.stack(prompt_t), jnp.stack(sample_t), jnp.stack(first_t)


def kernel(x_prompt, x_sample, state_gdn, state_conv, cache_win0, cache_win1, cache_win2, w_in_gdn, conv_w_gdn,
           a_log_gdn, dt_bias_gdn, norm_gdn, w_out_gdn, w_in_swa, w_out_swa, rel_bias, w_ffn_in, w_ffn_out,
           ln_mix_g, ln_mix_b, ln_ffn_g, ln_ffn_b):
    batch, seq, _ = x_prompt.shape
    nreq, ntok, _ = x_sample.shape
    caches = (cache_win0, cache_win1, cache_win2)
    mp = batch * seq
    tm_p = 1024
    tm_s = nreq * SAMPLE_ROWS

    xp = x_prompt.reshape(mp, D_MODEL)
    xs = jnp.pad(x_sample, ((0, 0), (0, SAMPLE_ROWS - ntok), (0, 0))).reshape(tm_s, D_MODEL)
    bias_p, bias_s, bias_first = _bias_tables(rel_bias, ntok)

    gdn_p, gdn_s, conv_p, conv_s = [], [], [], []
    win_p = [[] for _ in range(SWA_GROUPS)]
    win_s = [[] for _ in range(SWA_GROUPS)]
    for i in range(DEPTH):
        j = i // N_MIXERS
        ln_g, ln_b = ln_mix_g[i][None], ln_mix_b[i][None]
        if i % N_MIXERS == 0:
            w = w_in_gdn[j]
            w_qkvz = w[:, :GDN_QKVZ_DIM].astype(bf16)
            w_a = w[:, GDN_QKVZ_DIM:GDN_QKVZ_DIM + GDN_V_HEADS]
            w_b = w[:, GDN_QKVZ_DIM + GDN_V_HEADS:]
            fill = jnp.zeros((D_MODEL, LANES - 3 * GDN_V_HEADS), f32)
            w_ab = jnp.concatenate([w_a, w_b, w_a, fill], axis=1).astype(bf16)
            zpad = jnp.zeros((LANES - 3 * GDN_V_HEADS,), f32)
            zh = jnp.zeros((GDN_V_HEADS,), f32)
            alog = jnp.concatenate([a_log_gdn[j], zh, a_log_gdn[j], zpad])[None]
            dtb = jnp.concatenate([dt_bias_gdn[j], zh, dt_bias_gdn[j], zpad])[None]
            conv_w = conv_w_gdn[j]
            norm_w = norm_gdn[j][None]
            w_out = w_out_gdn[j].astype(bf16)

            pp = _proj(xp, w_qkvz, tm_p, 1024, f"gdn_in_p{j}")
            ps = _proj(xs, w_qkvz, tm_s, 1024, f"gdn_in_s{j}")
            gb_p, gbt_p = _gates(xp, w_ab, alog, dtb, tm_p, f"gdn_gates_p{j}")
            gb_s, _ = _gates(xs, w_ab, alog, dtb, tm_s, f"gdn_gates_s{j}")
            o_p, st_p = _gdn_prompt(pp, gb_p, gbt_p, conv_w, norm_w, batch, seq, f"gdn_prompt{j}")
            cbuf = jnp.pad(state_conv[j], ((0, 0), (0, SAMPLE_ROWS - (GDN_CONV_W - 1)), (0, 0)))
            o_s, st_s = _gdn_sample(ps, cbuf.reshape(tm_s, GDN_CONV_DIM), gb_s, conv_w, norm_w, state_gdn, j, 8,
                                    f"gdn_sample{j}")
            xp = _mm_ln(o_p, w_out, xp, ln_g, ln_b, tm_p, f"gdn_out_p{j}")
            xs = _mm_ln(o_s, w_out, xs, ln_g, ln_b, tm_s, f"gdn_out_s{j}")
            gdn_p.append(st_p)
            gdn_s.append(st_s)
            conv_p.append(pp.reshape(batch, seq, GDN_QKVZ_DIM)[:, seq - (GDN_CONV_W - 1):, :GDN_CONV_DIM])
            conv_s.append(ps.reshape(nreq, SAMPLE_ROWS, GDN_QKVZ_DIM)[:, ntok - (GDN_CONV_W - 1):ntok, :GDN_CONV_DIM])
        else:
            w_in = w_in_swa[j].astype(bf16)
            w_out = w_out_swa[j].astype(bf16)
            pp = _proj(xp, w_in, tm_p, 1024, f"swa_in_p{j}")
            ps = _proj(xs, w_in, tm_s, 1024, f"swa_in_s{j}")
            a_p = _swa_prompt(pp, bias_p, batch, seq, f"swa_prompt{j}")
            ps3 = ps.reshape(nreq, SAMPLE_ROWS, SWA_IN_DIM)
            qkv_s = ps3[:, :ntok].reshape(nreq, ntok, SWA_GROUPS, 3, SWA_HEADS, SWA_HEAD_DIM)
            a_s = _swa_sample(qkv_s, caches, j, bias_s, bias_first, f"swa_sample{j}")
            a_s = jnp.pad(a_s.reshape(nreq, ntok, SWA_INNER), ((0, 0), (0, SAMPLE_ROWS - ntok), (0, 0)))
            a_s = a_s.reshape(tm_s, SWA_INNER)
            xp = _mm_ln(a_p, w_out, xp, ln_g, ln_b, tm_p, f"swa_out_p{j}")
            xs = _mm_ln(a_s, w_out, xs, ln_g, ln_b, tm_s, f"swa_out_s{j}")
            pp3 = pp.reshape(batch, seq, SWA_IN_DIM)
            for g in range(SWA_GROUPS):
                keep = min(SWA_KEYS * SWA_DILATIONS[g], seq)
                lo, hi = (3 * g + 1) * SWA_INNER, (3 * g + 3) * SWA_INNER
                win_p[g].append(pp3[:, seq - keep:, lo:hi].reshape(batch, keep, 2, SWA_HEADS, SWA_HEAD_DIM))
                win_s[g].append(qkv_s[:, :, g, 1:])
        w_fi = w_ffn_in[i].astype(bf16)
        w_fo = w_ffn_out[i].astype(bf16)
        xp = _ffn(xp, w_fi, w_fo, ln_ffn_g[i][None], ln_ffn_b[i][None], tm_p, f"ffn_p{i}")
        xs = _ffn(xs, w_fi, w_fo, ln_ffn_g[i][None], ln_ffn_b[i][None], tm_s, f"ffn_s{i}")

    y_p = xp.reshape(batch, seq, D_MODEL)
    y_s = xs.reshape(nreq, SAMPLE_ROWS, D_MODEL)[:, :ntok]
    return (y_p, y_s, jnp.stack(gdn_p), jnp.stack(conv_p),
            jnp.stack(win_p[0]), jnp.stack(win_p[1]), jnp.stack(win_p[2]),
            jnp.stack(gdn_s), jnp.stack(conv_s),
            jnp.stack(win_s[0]), jnp.stack(win_s[1]), jnp.stack(win_s[2]))
```

```python
import functools
import math

import jax
import jax.numpy as jnp
from jax import lax
from jax.experimental import pallas as pl
from jax.experimental.pallas import tpu as pltpu

f32 = jnp.float32
bf16 = jnp.bfloat16

D_MODEL = 1024
DEPTH = 4
N_MIXERS = 2
GDN_QK_HEADS = 8
GDN_V_HEADS = 16
GDN_DK = 128
GDN_DV = 128
GDN_CONV_W = 4
GDN_QK_DIM = GDN_QK_HEADS * GDN_DK
GDN_V_DIM = GDN_V_HEADS * GDN_DV
GDN_CONV_DIM = 2 * GDN_QK_DIM + GDN_V_DIM
GDN_QKVZ_DIM = GDN_CONV_DIM + GDN_V_DIM
SWA_DILATIONS = (1, 4, 16)
SWA_GROUPS = 3
SWA_HEADS = 16
SWA_HEAD_DIM = 64
SWA_KEYS = 128
SWA_INNER = SWA_HEADS * SWA_HEAD_DIM
SWA_IN_DIM = SWA_GROUPS * 3 * SWA_INNER
N_BUCKETS = 32
MAX_DISTANCE = 2048
D_FF = 2816
DEEPNORM_ALPHA = (2 * DEPTH) ** 0.25
LN_EPS = 1e-5
RMS_EPS = 1e-6
L2_EPS = 1e-6

LANES = 128
SUBLANES = 8
VMEM_LIMIT = 56 * 1024 * 1024

CHUNK = 128
PREP_CHUNKS = 8
OUT_CHUNKS = 4
ATT_BLOCKS = 4
REQ_LOCK = 4
SAMPLE_ROWS = 8
FFN_TILE = 256


def _cparams(*sem):
    return pltpu.CompilerParams(dimension_semantics=sem, vmem_limit_bytes=VMEM_LIMIT)


def _dot(a, b):
    return jnp.dot(a, b, preferred_element_type=f32)


def _dot_nt(a, b):
    return lax.dot_general(a, b, (((1,), (1,)), ((), ())), preferred_element_type=f32)


def _sigmoid(x):
    return 1.0 / (1.0 + jnp.exp(-x))


def _silu(x):
    return x * _sigmoid(x)


def _layer_norm(y, g, b):
    mu = jnp.mean(y, axis=-1, keepdims=True)
    yc = y - mu
    var = jnp.mean(yc * yc, axis=-1, keepdims=True)
    return yc * lax.rsqrt(var + LN_EPS) * g + b


def _proj_body(x_ref, w_ref, o_ref, xb_ref):
    @pl.when(pl.program_id(1) == 0)
    def _():
        xb_ref[...] = x_ref[...].astype(bf16)

    o_ref[...] = _dot(xb_ref[...], w_ref[...]).astype(o_ref.dtype)


def _proj(x, w, tm, tn, name):
    m, k = x.shape
    n = w.shape[1]
    return pl.pallas_call(
        _proj_body,
        out_shape=jax.ShapeDtypeStruct((m, n), f32),
        grid=(m // tm, n // tn),
        in_specs=[pl.BlockSpec((tm, k), lambda i, j: (i, 0)),
                  pl.BlockSpec((k, tn), lambda i, j: (0, j))],
        out_specs=pl.BlockSpec((tm, tn), lambda i, j: (i, j)),
        scratch_shapes=[pltpu.VMEM((tm, k), bf16)],
        compiler_params=_cparams("parallel", "arbitrary"),
        name=name,
    )(x, w)


def _mm_ln_body(a_ref, w_ref, x_ref, g_ref, b_ref, o_ref):
    h = _dot(a_ref[...].astype(bf16), w_ref[...])
    o_ref[...] = _layer_norm(DEEPNORM_ALPHA * x_ref[...] + h, g_ref[...], b_ref[...])


def _mm_ln(a, w, x, g, b, tm, name):
    m, k = a.shape
    return pl.pallas_call(
        _mm_ln_body,
        out_shape=jax.ShapeDtypeStruct((m, D_MODEL), f32),
        grid=(m // tm,),
        in_specs=[pl.BlockSpec((tm, k), lambda i: (i, 0)),
                  pl.BlockSpec((k, D_MODEL), lambda i: (0, 0)),
                  pl.BlockSpec((tm, D_MODEL), lambda i: (i, 0)),
                  pl.BlockSpec((1, D_MODEL), lambda i: (0, 0)),
                  pl.BlockSpec((1, D_MODEL), lambda i: (0, 0))],
        out_specs=pl.BlockSpec((tm, D_MODEL), lambda i: (i, 0)),
        compiler_params=_cparams("parallel"),
        name=name,
    )(a, w, x, g, b)


def _ffn_body(x_ref, w1_ref, w2_ref, wo_ref, g_ref, b_ref, o_ref, xb_ref, acc_ref):
    j = pl.program_id(1)

    @pl.when(j == 0)
    def _():
        xb_ref[...] = x_ref[...].astype(bf16)
        acc_ref[...] = jnp.zeros_like(acc_ref)

    xb = xb_ref[...]
    act = _silu(_dot(xb, w1_ref[...])) * _dot(xb, w2_ref[...])
    acc_ref[...] += _dot(act.astype(bf16), wo_ref[...])

    @pl.when(j == pl.num_programs(1) - 1)
    def _():
        o_ref[...] = _layer_norm(DEEPNORM_ALPHA * x_ref[...] + acc_ref[...], g_ref[...], b_ref[...])


def _ffn(x, w_in, w_out, g, b, tm, name):
    m = x.shape[0]
    nf = D_FF // FFN_TILE
    return pl.pallas_call(
        _ffn_body,
        out_shape=jax.ShapeDtypeStruct((m, D_MODEL), f32),
        grid=(m // tm, nf),
        in_specs=[pl.BlockSpec((tm, D_MODEL), lambda i, j: (i, 0)),
                  pl.BlockSpec((D_MODEL, FFN_TILE), lambda i, j: (0, j)),
                  pl.BlockSpec((D_MODEL, FFN_TILE), lambda i, j: (0, j + nf)),
                  pl.BlockSpec((FFN_TILE, D_MODEL), lambda i, j: (j, 0)),
                  pl.BlockSpec((1, D_MODEL), lambda i, j: (0, 0)),
                  pl.BlockSpec((1, D_MODEL), lambda i, j: (0, 0))],
        out_specs=pl.BlockSpec((tm, D_MODEL), lambda i, j: (i, 0)),
        scratch_shapes=[pltpu.VMEM((tm, D_MODEL), bf16), pltpu.VMEM((tm, D_MODEL), f32)],
        compiler_params=_cparams("parallel", "arbitrary"),
        name=name,
    )(x, w_in, w_in, w_out, g, b)


def _gates_body(x_ref, w_ref, al_ref, dt_ref, gb_ref, gbt_ref):
    ab = _dot(x_ref[...].astype(bf16), w_ref[...])
    a = ab + dt_ref[...]
    softplus = jnp.maximum(a, 0.0) + jnp.log1p(jnp.exp(-jnp.abs(a)))
    g = -jnp.exp(al_ref[...]) * softplus
    beta = _sigmoid(ab)
    tm = ab.shape[0]
    row = lax.broadcasted_iota(jnp.int32, (CHUNK, CHUNK), 0)
    col = lax.broadcasted_iota(jnp.int32, (CHUNK, CHUNK), 1)
    tril = (row >= col).astype(f32)
    gc = jnp.concatenate(
        [jnp.dot(tril, g[c * CHUNK:(c + 1) * CHUNK], precision=lax.Precision.HIGHEST,
                 preferred_element_type=f32) for c in range(tm // CHUNK)], axis=0)
    lane = lax.broadcasted_iota(jnp.int32, (1, LANES), 1)
    gb = jnp.where(lane < GDN_V_HEADS, gc, jnp.where(lane < 2 * GDN_V_HEADS, beta, g))
    gb_ref[...] = gb
    gbt_ref[...] = gb.T


def _gates(x, w_ab, alog, dtb, tm, name):
    m = x.shape[0]
    return pl.pallas_call(
        _gates_body,
        out_shape=(jax.ShapeDtypeStruct((m, LANES), f32), jax.ShapeDtypeStruct((LANES, m), f32)),
        grid=(m // tm,),
        in_specs=[pl.BlockSpec((tm, D_MODEL), lambda i: (i, 0)),
                  pl.BlockSpec((D_MODEL, LANES), lambda i: (0, 0)),
                  pl.BlockSpec((1, LANES), lambda i: (0, 0)),
                  pl.BlockSpec((1, LANES), lambda i: (0, 0))],
        out_specs=(pl.BlockSpec((tm, LANES), lambda i: (i, 0)),
                   pl.BlockSpec((LANES, tm), lambda i: (0, i))),
        compiler_params=_cparams("parallel"),
        name=name,
    )(x, w_ab, alog, dtb)


def _lane_column(x, lane, idx):
    return jnp.sum(jnp.where(lane == idx, x, 0.0), axis=1, keepdims=True)


def _l2norm(x):
    return x * lax.rsqrt(jnp.sum(x * x, axis=-1, keepdims=True) + L2_EPS)


def _gated_rmsnorm(o, nw, z):
    return o * lax.rsqrt(jnp.mean(o * o, axis=-1, keepdims=True) + RMS_EPS) * nw * _silu(z)


def _unit_lower_inverses(neg_ls, eye, row, col):
    def same_block(size):
        return (row // size) == (col // size)

    base = 2 * SUBLANES
    inside = same_block(base)
    powers = [jnp.where(inside, n, 0.0) for n in neg_ls]
    invs = [eye + p for p in powers]
    span = 2
    while span < base:
        pbs = [p.astype(bf16) for p in powers]
        powers = [_dot(pb, pb) for pb in pbs]
        invs = [inv + _dot(inv.astype(bf16), p.astype(bf16)) for inv, p in zip(invs, powers)]
        span *= 2
    size = base
    while size < CHUNK:
        wider = same_block(2 * size)
        offs = [jnp.where(wider & ~inside, n, 0.0).astype(bf16) for n in neg_ls]
        inv_bs = [inv.astype(bf16) for inv in invs]
        mids = [_dot(off, inv_b).astype(bf16) for off, inv_b in zip(offs, inv_bs)]
        invs = [inv + _dot(inv_b, mid) for inv, inv_b, mid in zip(invs, inv_bs, mids)]
        inside = wider
        size *= 2
    return invs


def _gdn_prompt_body(q_ref, k_ref, v_ref, z_ref, gb_ref, gbt_ref, cwq_ref, cwk_ref, cwv_ref, nw_ref,
                     o_ref, st_ref, sb_scr, ob_scr, sk_scr, oq_scr, st_scr, el_scr, s_scr):
    hq = pl.program_id(1)
    nchunk = q_ref.shape[0] // CHUNK
    row = lax.broadcasted_iota(jnp.int32, (CHUNK, CHUNK), 0)
    col = lax.broadcasted_iota(jnp.int32, (CHUNK, CHUNK), 1)
    tri = row >= col
    strict = row > col
    eye = (row == col).astype(f32)
    lane = lax.broadcasted_iota(jnp.int32, (1, LANES), 1)
    rows8 = lax.broadcasted_iota(jnp.int32, (SUBLANES, 1), 0)
    cwq = cwq_ref[...]
    cwk = cwk_ref[...]
    cwv = cwv_ref[...]
    nw = nw_ref[...]

    tile0 = pl.multiple_of((2 * hq) // SUBLANES * SUBLANES, SUBLANES)

    def conv_silu(ref, c, r0, cw):
        x = ref[pl.ds(r0, CHUNK), :]
        p0 = pl.multiple_of(jnp.maximum(r0 - SUBLANES, 0), SUBLANES)
        prev8 = jnp.where(c > 0, ref[pl.ds(p0, SUBLANES), :], 0.0)
        acc = x * cw[GDN_CONV_W - 1:GDN_CONV_W]
        for s in range(1, GDN_CONV_W):
            rolled = pltpu.roll(x, s, 0)
            top = jnp.where(rows8 < s, pltpu.roll(prev8, s, 0), rolled[:SUBLANES])
            shifted = jnp.concatenate([top, rolled[SUBLANES:]], axis=0)
            acc = acc + shifted * cw[GDN_CONV_W - 1 - s:GDN_CONV_W - s]
        return _silu(acc)

    def prepare(i, carry):
        chains = []
        for cc in range(PREP_CHUNKS):
            c = i * PREP_CHUNKS + cc
            r0 = pl.multiple_of(c * CHUNK, CHUNK)
            qn = _l2norm(conv_silu(q_ref, c, r0, cwq)) * GDN_DK ** -0.5
            kn = _l2norm(conv_silu(k_ref, c, r0, cwk))
            vc = conv_silu(v_ref, c, r0, cwv)
            kb = kn.astype(bf16)
            kk = _dot_nt(kb, kb)
            qk = _dot_nt(qn.astype(bf16), kb)
            k_t = kn.T
            gbc = gb_ref[pl.ds(r0, CHUNK), :]
            gbt8 = gbt_ref[pl.ds(tile0, SUBLANES), pl.ds(r0, CHUNK)]
            for hh in range(2):
                hv = 2 * hq + hh
                gc_col = _lane_column(gbc, lane, hv)
                beta_col = _lane_column(gbc, lane, GDN_V_HEADS + hv)
                gc_row = jnp.sum(jnp.where(rows8 == hv % SUBLANES, gbt8, 0.0), axis=0, keepdims=True)
                decay = jnp.where(tri, jnp.exp(jnp.where(tri, gc_col - gc_row, 0.0)), 0.0)
                g_last = gc_row[:, CHUNK - 1:CHUNK]
                el_scr[hh, c] = jnp.broadcast_to(jnp.exp(g_last), (SUBLANES, GDN_DV))
                vh = vc[:, hh * GDN_DV:(hh + 1) * GDN_DV]
                chains.append(dict(
                    hh=hh, c=c,
                    neg_l=jnp.where(strict, -(beta_col * kk * decay), 0.0),
                    vb=(vh * beta_col).astype(bf16),
                    kbg=(kn * (beta_col * jnp.exp(gc_col))).astype(bf16),
                    a=jnp.where(tri, qk * decay, 0.0).astype(bf16),
                    qg=qn * jnp.exp(gc_col),
                    kd=(k_t * jnp.exp(g_last - gc_row)).astype(bf16)))
        invs = _unit_lower_inverses([ch["neg_l"] for ch in chains], eye, row, col)
        inv_bs = [inv.astype(bf16) for inv in invs]
        us = [_dot(inv_b, ch["vb"]).astype(bf16) for ch, inv_b in zip(chains, inv_bs)]
        ws = [_dot(inv_b, ch["kbg"]).astype(bf16) for ch, inv_b in zip(chains, inv_bs)]
        for ch, u, w in zip(chains, us, ws):
            hh, c = ch["hh"], ch["c"]
            sk_scr[hh, c] = _dot(ch["kd"], w).astype(bf16)
            sb_scr[hh, c] = _dot(ch["kd"], u)
            oq_scr[hh, c] = (ch["qg"] - _dot(ch["a"], w)).astype(bf16)
            ob_scr[hh, c] = _dot(ch["a"], u)
        return carry

    def advance(c, carry):
        for hh in range(2):
            state = s_scr[hh]
            state_b = state.astype(bf16)
            st_scr[hh, c] = state_b
            s_scr[hh] = state * el_scr[hh, c][0:1] + (sb_scr[hh, c] - _dot(sk_scr[hh, c], state_b))
        return carry

    def emit(i, carry):
        os_ = []
        for cc in range(OUT_CHUNKS):
            c = i * OUT_CHUNKS + cc
            for hh in range(2):
                os_.append((c, hh, _dot(oq_scr[hh, c], st_scr[hh, c]) + ob_scr[hh, c]))
        for cc in range(OUT_CHUNKS):
            c = i * OUT_CHUNKS + cc
            r0 = pl.multiple_of(c * CHUNK, CHUNK)
            zc = z_ref[pl.ds(r0, CHUNK), :]
            outs = [_gated_rmsnorm(o, nw, zc[:, hh * GDN_DV:(hh + 1) * GDN_DV])
                    for (c2, hh, o) in os_[2 * cc:2 * cc + 2]]
            o_ref[pl.ds(r0, CHUNK), :] = jnp.concatenate(outs, axis=1).astype(o_ref.dtype)
        return carry

    lax.fori_loop(0, nchunk // PREP_CHUNKS, prepare, 0)
    s_scr[...] = jnp.zeros_like(s_scr)
    lax.fori_loop(0, nchunk, advance, 0)
    st_ref[0] = s_scr[...]
    lax.fori_loop(0, nchunk // OUT_CHUNKS, emit, 0)


def _gdn_prompt(p, gb, gbt, conv_w, norm_w, batch, seq, name):
    nqk = GDN_QK_HEADS
    zoff = GDN_CONV_DIM // (2 * GDN_DV)
    nchunk = seq // CHUNK
    tiles_bf = pltpu.VMEM((2, nchunk, CHUNK, CHUNK), bf16)
    tiles_f32 = pltpu.VMEM((2, nchunk, CHUNK, CHUNK), f32)
    return pl.pallas_call(
        _gdn_prompt_body,
        out_shape=(jax.ShapeDtypeStruct((batch * seq, GDN_V_DIM), bf16),
                   jax.ShapeDtypeStruct((batch, GDN_V_HEADS, GDN_DK, GDN_DV), f32)),
        grid=(batch, nqk),
        in_specs=[pl.BlockSpec((seq, GDN_DK), lambda b, h: (b, h)),
                  pl.BlockSpec((seq, GDN_DK), lambda b, h: (b, nqk + h)),
                  pl.BlockSpec((seq, 2 * GDN_DV), lambda b, h: (b, nqk + h)),
                  pl.BlockSpec((seq, 2 * GDN_DV), lambda b, h: (b, zoff + h)),
                  pl.BlockSpec((seq, LANES), lambda b, h: (b, 0)),
                  pl.BlockSpec((LANES, seq), lambda b, h: (0, b)),
                  pl.BlockSpec((GDN_CONV_W, GDN_DK), lambda b, h: (0, h)),
                  pl.BlockSpec((GDN_CONV_W, GDN_DK), lambda b, h: (0, nqk + h)),
                  pl.BlockSpec((GDN_CONV_W, 2 * GDN_DV), lambda b, h: (0, nqk + h)),
                  pl.BlockSpec((1, GDN_DV), lambda b, h: (0, 0))],
        out_specs=(pl.BlockSpec((seq, 2 * GDN_DV), lambda b, h: (b, h)),
                   pl.BlockSpec((1, 2, GDN_DK, GDN_DV), lambda b, h: (b, h, 0, 0))),
        scratch_shapes=[tiles_f32, tiles_f32, tiles_bf, tiles_bf, tiles_bf,
                        pltpu.VMEM((2, nchunk, SUBLANES, GDN_DV), f32),
                        pltpu.VMEM((2, GDN_DK, GDN_DV), f32)],
        compiler_params=_cparams("parallel", "parallel"),
        name=name,
    )(p, p, p, p, gb, gbt, conv_w, conv_w, conv_w, norm_w)


def _gdn_sample_body(q_ref, k_ref, v_ref, z_ref, aq_ref, ak_ref, av_ref, gb_ref, cwq_ref, cwk_ref, cwv_ref,
                     nw_ref, st_ref, o_ref, sto_ref, q_scr, k_scr, v_scr, o_scr):
    hq = pl.program_id(1)
    rows = q_ref.shape[0]
    nreq = rows // SAMPLE_ROWS
    tok = lax.broadcasted_iota(jnp.int32, (rows, 1), 0) % SAMPLE_ROWS
    lane = lax.broadcasted_iota(jnp.int32, (1, LANES), 1)
    rows8 = lax.broadcasted_iota(jnp.int32, (SAMPLE_ROWS, 1), 0)
    nw = nw_ref[...]
    hist = GDN_CONV_W - 1

    def conv_silu(a, x, cw):
        acc = jnp.zeros_like(x)
        for w in range(GDN_CONV_W):
            ra = a if w == 0 else pltpu.roll(a, rows - w, 0)
            rx = x if w == hist else pltpu.roll(x, hist - w, 0)
            acc = acc + jnp.where(tok + w < hist, ra, rx) * cw[w:w + 1]
        return _silu(acc)

    q_scr[...] = _l2norm(conv_silu(aq_ref[...], q_ref[...], cwq_ref[...])) * GDN_DK ** -0.5
    k_scr[...] = _l2norm(conv_silu(ak_ref[...], k_ref[...], cwk_ref[...]))
    v_scr[...] = conv_silu(av_ref[...], v_ref[...], cwv_ref[...])
    pad = jnp.zeros((LANES - 2 * SAMPLE_ROWS, GDN_DK), f32)

    def requests(i, carry):
        work = []
        for rr in range(REQ_LOCK):
            bi = i * REQ_LOCK + rr
            r0 = pl.multiple_of(bi * SAMPLE_ROWS, SAMPLE_ROWS)
            k8 = k_scr[pl.ds(r0, SAMPLE_ROWS), :]
            q8 = q_scr[pl.ds(r0, SAMPLE_ROWS), :]
            v8 = v_scr[pl.ds(r0, SAMPLE_ROWS), :]
            gb8 = gb_ref[pl.ds(r0, SAMPLE_ROWS), :]
            kq_t = jnp.concatenate([k8, q8, pad], axis=0).T
            for hh in range(2):
                hv = 2 * hq + hh
                work.append(dict(bi=bi, hh=hh, r0=r0, kq_t=kq_t, v=v8[:, hh * GDN_DV:(hh + 1) * GDN_DV],
                                 g=_lane_column(gb8, lane, 2 * GDN_V_HEADS + hv),
                                 beta=_lane_column(gb8, lane, GDN_V_HEADS + hv),
                                 state=st_ref[bi, hh], o=jnp.zeros((SAMPLE_ROWS, GDN_DV), f32)))
        for t in range(GDN_CONV_W):
            for w in work:
                w["state"] = w["state"] * jnp.exp(w["g"][t:t + 1])
            for w in work:
                k_col = w["kq_t"][:, t:t + 1]
                ks = jnp.sum(w["state"] * k_col, axis=0, keepdims=True)
                delta = (w["v"][t:t + 1] - ks) * w["beta"][t:t + 1]
                w["state"] = w["state"] + k_col * delta
            for w in work:
                q_col = w["kq_t"][:, SAMPLE_ROWS + t:SAMPLE_ROWS + t + 1]
                o_t = jnp.sum(w["state"] * q_col, axis=0, keepdims=True)
                w["o"] = jnp.where(rows8 == t, o_t, w["o"])
        for w in work:
            sto_ref[w["bi"], w["hh"]] = w["state"]
        for w0, w1 in zip(work[0::2], work[1::2]):
            z8 = z_ref[pl.ds(w0["r0"], SAMPLE_ROWS), :]
            o_scr[pl.ds(w0["r0"], SAMPLE_ROWS), :] = jnp.concatenate(
                [_gated_rmsnorm(w["o"], nw, z8[:, w["hh"] * GDN_DV:(w["hh"] + 1) * GDN_DV]) for w in (w0, w1)], axis=1)
        return carry

    lax.fori_loop(0, nreq // REQ_LOCK, requests, 0)
    o_ref[...] = o_scr[...].astype(o_ref.dtype)


def _gdn_sample(p, conv_buf, gb, conv_w, norm_w, state, layer, nreq_blk, name):
    assert GDN_CONV_W == 4 and SAMPLE_ROWS == 8
    nqk = GDN_QK_HEADS
    zoff = GDN_CONV_DIM // (2 * GDN_DV)
    nreq = state.shape[1]
    rows = nreq_blk * SAMPLE_ROWS
    st_spec = pl.BlockSpec((None, nreq_blk, 2, GDN_DK, GDN_DV), lambda i, h: (layer, i, h, 0, 0))
    return pl.pallas_call(
        _gdn_sample_body,
        out_shape=(jax.ShapeDtypeStruct((nreq * SAMPLE_ROWS, GDN_V_DIM), bf16),
                   jax.ShapeDtypeStruct((nreq, GDN_V_HEADS, GDN_DK, GDN_DV), f32)),
        grid=(nreq // nreq_blk, nqk),
        in_specs=[pl.BlockSpec((rows, GDN_DK), lambda i, h: (i, h)),
                  pl.BlockSpec((rows, GDN_DK), lambda i, h: (i, nqk + h)),
                  pl.BlockSpec((rows, 2 * GDN_DV), lambda i, h: (i, nqk + h)),
                  pl.BlockSpec((rows, 2 * GDN_DV), lambda i, h: (i, zoff + h)),
                  pl.BlockSpec((rows, GDN_DK), lambda i, h: (i, h)),
                  pl.BlockSpec((rows, GDN_DK), lambda i, h: (i, nqk + h)),
                  pl.BlockSpec((rows, 2 * GDN_DV), lambda i, h: (i, nqk + h)),
                  pl.BlockSpec((rows, LANES), lambda i, h: (i, 0)),
                  pl.BlockSpec((GDN_CONV_W, GDN_DK), lambda i, h: (0, h)),
                  pl.BlockSpec((GDN_CONV_W, GDN_DK), lambda i, h: (0, nqk + h)),
                  pl.BlockSpec((GDN_CONV_W, 2 * GDN_DV), lambda i, h: (0, nqk + h)),
                  pl.BlockSpec((1, GDN_DV), lambda i, h: (0, 0)),
                  st_spec],
        out_specs=(pl.BlockSpec((rows, 2 * GDN_DV), lambda i, h: (i, h)),
                   pl.BlockSpec((nreq_blk, 2, GDN_DK, GDN_DV), lambda i, h: (i, h, 0, 0))),
        scratch_shapes=[pltpu.VMEM((rows, GDN_DK), f32), pltpu.VMEM((rows, GDN_DK), f32),
                        pltpu.VMEM((rows, 2 * GDN_DV), f32), pltpu.VMEM((rows, 2 * GDN_DV), f32)],
        compiler_params=_cparams("parallel", "parallel"),
        name=name,
    )(p, p, p, p, conv_buf, conv_buf, conv_buf, gb, conv_w, conv_w, conv_w, norm_w, state)


def _swa_prompt_body(*refs):
    qkv_refs = refs[:3 * SWA_GROUPS]
    bias_ref, o_ref, o_scr, l_scr = refs[3 * SWA_GROUPS:]
    seq = o_ref.shape[0]
    lane = lax.broadcasted_iota(jnp.int32, (1, LANES), 1)
    head0 = lane < SWA_HEAD_DIM
    blk = SWA_KEYS
    for g, d in enumerate(SWA_DILATIONS):
        q_ref, k_ref, v_ref = qkv_refs[3 * g:3 * g + 3]
        nb = seq // (d * blk)

        def rows_of(start, d=d):
            return pl.ds(start, blk) if d == 1 else pl.ds(start, blk, stride=d)

        def blocks(i, carry, g=g, d=d, nb=nb, q_ref=q_ref, k_ref=k_ref, v_ref=v_ref, rows_of=rows_of):
            has_prev = nb > 1
            work = []
            for bb in range(ATT_BLOCKS):
                idx = i * ATT_BLOCKS + bb
                r = idx // nb
                b = idx % nb
                cur = rows_of(r + d * blk * b)
                q = q_ref[cur, :] * SWA_HEAD_DIM ** -0.5
                kc = k_ref[cur, :].astype(bf16)
                vc = v_ref[cur, :].astype(bf16)
                kp = vp = no_prev = None
                if has_prev:
                    prev = rows_of(r + d * blk * jnp.maximum(b - 1, 0))
                    no_prev = jnp.where(b == 0, -jnp.inf, 0.0).astype(f32)
                    kp = k_ref[prev, :].astype(bf16)
                    vp = v_ref[prev, :].astype(bf16)
                for hh in range(2):
                    qh = jnp.where(head0 if hh == 0 else ~head0, q, 0.0).astype(bf16)
                    work.append(dict(cur=cur, hh=hh, qh=qh, kc=kc, vc=vc, kp=kp, vp=vp, no_prev=no_prev))
            for w in work:
                w["s_c"] = _dot_nt(w["qh"], w["kc"]) + bias_ref[g, w["hh"]][:, blk:]
            if has_prev:
                for w in work:
                    w["s_p"] = _dot_nt(w["qh"], w["kp"]) + (bias_ref[g, w["hh"]][:, :blk] + w["no_prev"])
            for w in work:
                m = jnp.max(jnp.maximum(w["s_c"], w["s_p"]) if has_prev else w["s_c"], axis=-1, keepdims=True)
                p_c = jnp.exp(w["s_c"] - m)
                w["p_c"] = p_c.astype(bf16)
                if has_prev:
                    p_p = jnp.exp(w["s_p"] - m)
                    w["p_p"] = p_p.astype(bf16)
                    p_c = p_c + p_p
                w["m"], w["den"] = m, jnp.sum(p_c, axis=-1, keepdims=True)
            for w in work:
                w["pv"] = _dot(w["p_c"], w["vc"])
            if has_prev:
                for w in work:
                    w["pv"] = w["pv"] + _dot(w["p_p"], w["vp"])
            for w0, w1 in zip(work[0::2], work[1::2]):
                o_scr[g, w0["cur"], :] = jnp.where(head0, w0["pv"] / w0["den"], w1["pv"] / w1["den"])
                l_scr[g, w0["cur"], :] = jnp.where(head0, w0["m"] + jnp.log(w0["den"]), w1["m"] + jnp.log(w1["den"]))
            return carry

        lax.fori_loop(0, seq // (blk * ATT_BLOCKS), blocks, 0)

    def merge(c, carry):
        rr = pl.ds(pl.multiple_of(c * blk, blk), blk)
        ls = [l_scr[g, rr, :] for g in range(SWA_GROUPS)]
        mx = jnp.maximum(jnp.maximum(ls[0], ls[1]), ls[2])
        ws = [jnp.exp(l - mx) for l in ls]
        num = ws[0] * o_scr[0, rr, :] + ws[1] * o_scr[1, rr, :] + ws[2] * o_scr[2, rr, :]
        o_ref[rr, :] = (num / (ws[0] + ws[1] + ws[2])).astype(o_ref.dtype)
        return carry

    lax.fori_loop(0, seq // blk, merge, 0)


def _swa_prompt(p, bias_t, batch, seq, name):
    npair = SWA_INNER // LANES
    in_specs = []
    for g in range(SWA_GROUPS):
        for c in range(3):
            off = (g * 3 + c) * npair
            in_specs.append(pl.BlockSpec((seq, LANES), lambda b, h, off=off: (b, off + h)))
    in_specs.append(pl.BlockSpec((SWA_GROUPS, 2, SWA_KEYS, 2 * SWA_KEYS), lambda b, h: (0, h, 0, 0)))
    return pl.pallas_call(
        _swa_prompt_body,
        out_shape=jax.ShapeDtypeStruct((batch * seq, SWA_INNER), bf16),
        grid=(batch, npair),
        in_specs=in_specs,
        out_specs=pl.BlockSpec((seq, LANES), lambda b, h: (b, h)),
        scratch_shapes=[pltpu.VMEM((SWA_GROUPS, seq, LANES), f32), pltpu.VMEM((SWA_GROUPS, seq, LANES), f32)],
        compiler_params=_cparams("parallel", "parallel"),
        name=name,
    )(*([p] * (3 * SWA_GROUPS)), bias_t)


def _swa_sample_body(q_ref, c0_ref, c1_ref, c2_ref, tb0_ref, tb1_ref, tb2_ref, tn_ref, o_ref):
    nhead = q_ref.shape[2]
    nrow = nhead * SAMPLE_ROWS
    cache_refs = (c0_ref, c1_ref, c2_ref)
    tb_refs = (tb0_ref, tb1_ref, tb2_ref)
    heads = [slice(h * SAMPLE_ROWS, (h + 1) * SAMPLE_ROWS) for h in range(nhead)]
    qs, s_cache, s_new = [], [], []
    for g in range(SWA_GROUPS):
        q = (q_ref[g, 0].reshape(nrow, SWA_HEAD_DIM) * SWA_HEAD_DIM ** -0.5).astype(bf16)
        qs.append(q)
        s_cache.append(jnp.concatenate(
            [_dot(q[heads[h]], cache_refs[g][0, h].astype(bf16)) for h in range(nhead)], axis=0) + tb_refs[g][...])
    for g in range(SWA_GROUPS):
        k_new = q_ref[g, 1].reshape(nrow, SWA_HEAD_DIM).astype(bf16)
        s_new.append(_dot_nt(qs[g], k_new) + tn_ref[g])
    ms, dens, ps, pns = [], [], [], []
    for g in range(SWA_GROUPS):
        m = jnp.maximum(jnp.max(s_cache[g], axis=-1, keepdims=True), jnp.max(s_new[g], axis=-1, keepdims=True))
        p = jnp.exp(s_cache[g] - m)
        pn = jnp.exp(s_new[g] - m)
        ms.append(m)
        dens.append(jnp.sum(p, axis=-1, keepdims=True) + jnp.sum(pn, axis=-1, keepdims=True))
        ps.append(p.astype(bf16))
        pns.append(pn.astype(bf16))
    pvs = []
    for g in range(SWA_GROUPS):
        pvs.append(jnp.concatenate(
            [_dot_nt(ps[g][heads[h]], cache_refs[g][1, h].astype(bf16)) for h in range(nhead)], axis=0))
    for g in range(SWA_GROUPS):
        v_new = q_ref[g, 2].reshape(nrow, SWA_HEAD_DIM).astype(bf16)
        pvs[g] = pvs[g] + _dot(pns[g], v_new)
    o_g = [pv / den for pv, den in zip(pvs, dens)]
    l_g = [m + jnp.log(den) for m, den in zip(ms, dens)]
    mx = jnp.maximum(jnp.maximum(l_g[0], l_g[1]), l_g[2])
    ws = [jnp.exp(l - mx) for l in l_g]
    merged = (ws[0] * o_g[0] + ws[1] * o_g[1] + ws[2] * o_g[2]) / (ws[0] + ws[1] + ws[2])
    o_ref[...] = merged.reshape(nhead, SAMPLE_ROWS, SWA_HEAD_DIM)


def _swa_sample(qkv, caches_t, layer, tb, tn, name):
    nreq = qkv.shape[0]
    nhead = SWA_HEADS // 2
    nrow = nhead * SAMPLE_ROWS
    in_specs = [pl.BlockSpec((None, SWA_GROUPS, 3, nhead, SAMPLE_ROWS, SWA_HEAD_DIM),
                             lambda b, h: (b, 0, 0, h, 0, 0))]
    for c in caches_t:
        in_specs.append(pl.BlockSpec((None, None, 2, nhead, SWA_HEAD_DIM, c.shape[-1]),
                                     lambda b, h: (layer, b, 0, h, 0, 0)))
    for t in tb:
        in_specs.append(pl.BlockSpec((nrow, t.shape[-1]), lambda b, h: (h, 0)))
    in_specs.append(pl.BlockSpec((None, SWA_GROUPS, nrow, nrow), lambda b, h: (h, 0, 0, 0)))
    return pl.pallas_call(
        _swa_sample_body,
        out_shape=jax.ShapeDtypeStruct((nreq, SWA_HEADS, SAMPLE_ROWS, SWA_HEAD_DIM), f32),
        grid=(nreq, SWA_HEADS // nhead),
        in_specs=in_specs,
        out_specs=pl.BlockSpec((None, nhead, SAMPLE_ROWS, SWA_HEAD_DIM), lambda b, h: (b, h, 0, 0)),
        compiler_params=_cparams("parallel", "parallel"),
        name=name,
    )(qkv, *caches_t, *tb, tn)


def _t5_bucket(dist):
    max_exact = N_BUCKETS // 2
    dd = jnp.maximum(dist, 1).astype(f32)
    large = max_exact + (jnp.log(dd / max_exact) / math.log(MAX_DISTANCE / max_exact)
                         * (N_BUCKETS - max_exact)).astype(jnp.int32)
    return jnp.where(dist < max_exact, dist, jnp.minimum(large, N_BUCKETS - 1))


def _select_rows(table, idx):
    onehot = (idx[..., None] == jnp.arange(table.shape[0])).astype(f32)
    return jnp.einsum("...n,nc->...c", onehot, table, precision=lax.Precision.HIGHEST)


def _bias_tables(rel_bias, ntok, cache_lens):
    neg = -jnp.inf
    i = jnp.arange(SWA_KEYS)[:, None]
    jj = jnp.arange(2 * SWA_KEYS)[None, :]
    steps = SWA_KEYS + i - jj
    band = (steps >= 0) & (steps <= SWA_KEYS)
    t_row = jnp.where(jnp.arange(SAMPLE_ROWS) < ntok, jnp.arange(SAMPLE_ROWS), 0)[:, None]
    t_new = jnp.arange(SAMPLE_ROWS)[None, :]
    same_head = jnp.eye(SWA_HEADS, dtype=bool)[:, None, :, None]
    nhalf = SWA_HEADS // 2 * SAMPLE_ROWS
    prompt_t, cached_t, new_t = [], [], []
    for g, d in enumerate(SWA_DILATIONS):
        dist = jnp.arange(SWA_KEYS + 1, dtype=jnp.int32) * d
        bias_g = _select_rows(rel_bias[:, g * SWA_HEADS:(g + 1) * SWA_HEADS], _t5_bucket(dist))
        toeplitz = _select_rows(bias_g, jnp.clip(steps, 0, SWA_KEYS))
        prompt_t.append(jnp.where(band[None], jnp.transpose(toeplitz, (2, 0, 1)), neg))
        length = cache_lens[g]
        back = length + t_row - jnp.arange(length)[None, :]
        ok = (back % d == 0) & (back >= d) & (back <= SWA_KEYS * d)
        vals = _select_rows(bias_g, jnp.clip(back // d, 0, SWA_KEYS))
        cached = jnp.where(ok[None], jnp.transpose(vals, (2, 0, 1)), neg)
        cached_t.append(cached.reshape(SWA_HEADS * SAMPLE_ROWS, length))
        back = t_row - t_new
        ok = (back % d == 0) & (back >= 0) & (t_new < ntok)
        vals = _select_rows(bias_g, jnp.clip(back // d, 0, SWA_KEYS))
        fresh = jnp.where(ok[None], jnp.transpose(vals, (2, 0, 1)), neg)
        full = jnp.where(same_head, fresh[:, :, None, :], neg).reshape(2 * nhalf, 2 * nhalf)
        new_t.append(jnp.stack([full[:nhalf, :nhalf], full[nhalf:, nhalf:]]))
    return jnp.stack(prompt_t), cached_t, jnp.stack(new_t, axis=1)


def kernel(x_prompt, x_sample, state_gdn, state_conv, cache_win0, cache_win1, cache_win2, w_in_gdn, conv_w_gdn,
           a_log_gdn, dt_bias_gdn, norm_gdn, w_out_gdn, w_in_swa, w_out_swa, rel_bias, w_ffn_in, w_ffn_out,
           ln_mix_g, ln_mix_b, ln_ffn_g, ln_ffn_b):
    batch, seq, _ = x_prompt.shape
    nreq, ntok, _ = x_sample.shape
    caches = (cache_win0, cache_win1, cache_win2)
    mp = batch * seq
    tm_p = 1024
    tm_s = nreq * SAMPLE_ROWS

    xp = x_prompt.reshape(mp, D_MODEL)
    xs = jnp.pad(x_sample, ((0, 0), (0, SAMPLE_ROWS - ntok), (0, 0))).reshape(tm_s, D_MODEL)
    caches_t = tuple(jnp.transpose(c, (0, 1, 3, 4, 5, 2)) for c in caches)
    bias_p, bias_s, bias_new = _bias_tables(rel_bias, ntok, [c.shape[2] for c in caches])

    gdn_p, gdn_s, conv_p, conv_s = [], [], [], []
    win_p = [[] for _ in range(SWA_GROUPS)]
    win_s = [[] for _ in range(SWA_GROUPS)]
    for i in range(DEPTH):
        j = i // N_MIXERS
        ln_g, ln_b = ln_mix_g[i][None], ln_mix_b[i][None]
        if i % N_MIXERS == 0:
            w = w_in_gdn[j]
            w_qkvz = w[:, :GDN_QKVZ_DIM].astype(bf16)
            w_a = w[:, GDN_QKVZ_DIM:GDN_QKVZ_DIM + GDN_V_HEADS]
            w_b = w[:, GDN_QKVZ_DIM + GDN_V_HEADS:]
            fill = jnp.zeros((D_MODEL, LANES - 3 * GDN_V_HEADS), f32)
            w_ab = jnp.concatenate([w_a, w_b, w_a, fill], axis=1).astype(bf16)
            zpad = jnp.zeros((LANES - 3 * GDN_V_HEADS,), f32)
            zh = jnp.zeros((GDN_V_HEADS,), f32)
            alog = jnp.concatenate([a_log_gdn[j], zh, a_log_gdn[j], zpad])[None]
            dtb = jnp.concatenate([dt_bias_gdn[j], zh, dt_bias_gdn[j], zpad])[None]
            conv_w = conv_w_gdn[j]
            norm_w = norm_gdn[j][None]
            w_out = w_out_gdn[j].astype(bf16)

            pp = _proj(xp, w_qkvz, tm_p, 1024, f"gdn_in_p{j}")
            ps = _proj(xs, w_qkvz, tm_s, 1024, f"gdn_in_s{j}")
            gb_p, gbt_p = _gates(xp, w_ab, alog, dtb, tm_p, f"gdn_gates_p{j}")
            gb_s, _ = _gates(xs, w_ab, alog, dtb, tm_s, f"gdn_gates_s{j}")
            o_p, st_p = _gdn_prompt(pp, gb_p, gbt_p, conv_w, norm_w, batch, seq, f"gdn_prompt{j}")
            cbuf = jnp.pad(state_conv[j], ((0, 0), (0, SAMPLE_ROWS - (GDN_CONV_W - 1)), (0, 0)))
            o_s, st_s = _gdn_sample(ps, cbuf.reshape(tm_s, GDN_CONV_DIM), gb_s, conv_w, norm_w, state_gdn, j, 8,
                                    f"gdn_sample{j}")
            xp = _mm_ln(o_p, w_out, xp, ln_g, ln_b, tm_p, f"gdn_out_p{j}")
            xs = _mm_ln(o_s, w_out, xs, ln_g, ln_b, tm_s, f"gdn_out_s{j}")
            gdn_p.append(st_p)
            gdn_s.append(st_s)
            conv_p.append(pp.reshape(batch, seq, GDN_QKVZ_DIM)[:, seq - (GDN_CONV_W - 1):, :GDN_CONV_DIM])
            conv_s.append(ps.reshape(nreq, SAMPLE_ROWS, GDN_QKVZ_DIM)[:, ntok - (GDN_CONV_W - 1):ntok, :GDN_CONV_DIM])
        else:
            w_in = w_in_swa[j].astype(bf16)
            w_out = w_out_swa[j].astype(bf16)
            pp = _proj(xp, w_in, tm_p, 1024, f"swa_in_p{j}")
            ps = _proj(xs, w_in, tm_s, 1024, f"swa_in_s{j}")
            a_p = _swa_prompt(pp, bias_p, batch, seq, f"swa_prompt{j}")
            ps6 = ps.reshape(nreq, SAMPLE_ROWS, SWA_GROUPS, 3, SWA_HEADS, SWA_HEAD_DIM)
            qkv_s = jnp.transpose(ps6, (0, 2, 3, 4, 1, 5))
            a_s = _swa_sample(qkv_s, caches_t, j, bias_s, bias_new, f"swa_sample{j}")
            a_s = jnp.transpose(a_s, (0, 2, 1, 3)).reshape(tm_s, SWA_INNER)
            xp = _mm_ln(a_p, w_out, xp, ln_g, ln_b, tm_p, f"swa_out_p{j}")
            xs = _mm_ln(a_s, w_out, xs, ln_g, ln_b, tm_s, f"swa_out_s{j}")
            pp3 = pp.reshape(batch, seq, SWA_IN_DIM)
            for g in range(SWA_GROUPS):
                keep = min(SWA_KEYS * SWA_DILATIONS[g], seq)
                lo, hi = (3 * g + 1) * SWA_INNER, (3 * g + 3) * SWA_INNER
                win_p[g].append(pp3[:, seq - keep:, lo:hi].reshape(batch, keep, 2, SWA_HEADS, SWA_HEAD_DIM))
                win_s[g].append(jnp.transpose(qkv_s[:, g, 1:, :, :ntok], (0, 3, 1, 2, 4)))
        w_fi = w_ffn_in[i].astype(bf16)
        w_fo = w_ffn_out[i].astype(bf16)
        xp = _ffn(xp, w_fi, w_fo, ln_ffn_g[i][None], ln_ffn_b[i][None], tm_p, f"ffn_p{i}")
        xs = _ffn(xs, w_fi, w_fo, ln_ffn_g[i][None], ln_ffn_b[i][None], tm_s, f"ffn_s{i}")

    y_p = xp.reshape(batch, seq, D_MODEL)
    y_s = xs.reshape(nreq, SAMPLE_ROWS, D_MODEL)[:, :ntok]
    return (y_p, y_s, jnp.stack(gdn_p), jnp.stack(conv_p),
            jnp.stack(win_p[0]), jnp.stack(win_p[1]), jnp.stack(win_p[2]),
            jnp.stack(gdn_s), jnp.stack(conv_s),
            jnp.stack(win_s[0]), jnp.stack(win_s[1]), jnp.stack(win_s[2]))
```

```python
import functools
import math

import jax
import jax.numpy as jnp
from jax import lax
from jax.experimental import pallas as pl
from jax.experimental.pallas import tpu as pltpu

f32 = jnp.float32
bf16 = jnp.bfloat16

D_MODEL = 1024
DEPTH = 4
N_MIXERS = 2
GDN_QK_HEADS = 8
GDN_V_HEADS = 16
GDN_DK = 128
GDN_DV = 128
GDN_CONV_W = 4
GDN_QK_DIM = GDN_QK_HEADS * GDN_DK
GDN_V_DIM = GDN_V_HEADS * GDN_DV
GDN_CONV_DIM = 2 * GDN_QK_DIM + GDN_V_DIM
GDN_QKVZ_DIM = GDN_CONV_DIM + GDN_V_DIM
SWA_DILATIONS = (1, 4, 16)
SWA_GROUPS = 3
SWA_HEADS = 16
SWA_HEAD_DIM = 64
SWA_KEYS = 128
SWA_INNER = SWA_HEADS * SWA_HEAD_DIM
SWA_IN_DIM = SWA_GROUPS * 3 * SWA_INNER
N_BUCKETS = 32
MAX_DISTANCE = 2048
D_FF = 2816
DEEPNORM_ALPHA = (2 * DEPTH) ** 0.25
LN_EPS = 1e-5
RMS_EPS = 1e-6
L2_EPS = 1e-6

LANES = 128
SUBLANES = 8
VMEM_LIMIT = 56 * 1024 * 1024

CHUNK = 128
PREP_CHUNKS = 8
OUT_CHUNKS = 4
ATT_BLOCKS = 8
REQ_LOCK = 4
SAMPLE_ROWS = 8
FFN_TILE = 256


def _cparams(*sem):
    return pltpu.CompilerParams(dimension_semantics=sem, vmem_limit_bytes=VMEM_LIMIT)


def _dot(a, b):
    return jnp.dot(a, b, preferred_element_type=f32)


def _dot_nt(a, b):
    return lax.dot_general(a, b, (((1,), (1,)), ((), ())), preferred_element_type=f32)


def _sigmoid(x):
    return 1.0 / (1.0 + jnp.exp(-x))


def _silu(x):
    return x * _sigmoid(x)


def _layer_norm(y, g, b):
    mu = jnp.mean(y, axis=-1, keepdims=True)
    yc = y - mu
    var = jnp.mean(yc * yc, axis=-1, keepdims=True)
    return yc * lax.rsqrt(var + LN_EPS) * g + b


def _proj_body(x_ref, w_ref, o_ref, xb_ref):
    @pl.when(pl.program_id(1) == 0)
    def _():
        xb_ref[...] = x_ref[...].astype(bf16)

    o_ref[...] = _dot(xb_ref[...], w_ref[...]).astype(o_ref.dtype)


def _proj(x, w, tm, tn, name):
    m, k = x.shape
    n = w.shape[1]
    return pl.pallas_call(
        _proj_body,
        out_shape=jax.ShapeDtypeStruct((m, n), f32),
        grid=(m // tm, n // tn),
        in_specs=[pl.BlockSpec((tm, k), lambda i, j: (i, 0)),
                  pl.BlockSpec((k, tn), lambda i, j: (0, j))],
        out_specs=pl.BlockSpec((tm, tn), lambda i, j: (i, j)),
        scratch_shapes=[pltpu.VMEM((tm, k), bf16)],
        compiler_params=_cparams("parallel", "arbitrary"),
        name=name,
    )(x, w)


def _mm_ln_body(a_ref, w_ref, x_ref, g_ref, b_ref, o_ref):
    h = _dot(a_ref[...].astype(bf16), w_ref[...])
    o_ref[...] = _layer_norm(DEEPNORM_ALPHA * x_ref[...] + h, g_ref[...], b_ref[...])


def _mm_ln(a, w, x, g, b, tm, name):
    m, k = a.shape
    return pl.pallas_call(
        _mm_ln_body,
        out_shape=jax.ShapeDtypeStruct((m, D_MODEL), f32),
        grid=(m // tm,),
        in_specs=[pl.BlockSpec((tm, k), lambda i: (i, 0)),
                  pl.BlockSpec((k, D_MODEL), lambda i: (0, 0)),
                  pl.BlockSpec((tm, D_MODEL), lambda i: (i, 0)),
                  pl.BlockSpec((1, D_MODEL), lambda i: (0, 0)),
                  pl.BlockSpec((1, D_MODEL), lambda i: (0, 0))],
        out_specs=pl.BlockSpec((tm, D_MODEL), lambda i: (i, 0)),
        compiler_params=_cparams("parallel"),
        name=name,
    )(a, w, x, g, b)


def _ffn_body(x_ref, w1_ref, w2_ref, wo_ref, g_ref, b_ref, o_ref, xb_ref, acc_ref):
    j = pl.program_id(1)

    @pl.when(j == 0)
    def _():
        xb_ref[...] = x_ref[...].astype(bf16)
        acc_ref[...] = jnp.zeros_like(acc_ref)

    xb = xb_ref[...]
    act = _silu(_dot(xb, w1_ref[...])) * _dot(xb, w2_ref[...])
    acc_ref[...] += _dot(act.astype(bf16), wo_ref[...])

    @pl.when(j == pl.num_programs(1) - 1)
    def _():
        o_ref[...] = _layer_norm(DEEPNORM_ALPHA * x_ref[...] + acc_ref[...], g_ref[...], b_ref[...])


def _ffn(x, w_in, w_out, g, b, tm, name):
    m = x.shape[0]
    nf = D_FF // FFN_TILE
    return pl.pallas_call(
        _ffn_body,
        out_shape=jax.ShapeDtypeStruct((m, D_MODEL), f32),
        grid=(m // tm, nf),
        in_specs=[pl.BlockSpec((tm, D_MODEL), lambda i, j: (i, 0)),
                  pl.BlockSpec((D_MODEL, FFN_TILE), lambda i, j: (0, j)),
                  pl.BlockSpec((D_MODEL, FFN_TILE), lambda i, j: (0, j + nf)),
                  pl.BlockSpec((FFN_TILE, D_MODEL), lambda i, j: (j, 0)),
                  pl.BlockSpec((1, D_MODEL), lambda i, j: (0, 0)),
                  pl.BlockSpec((1, D_MODEL), lambda i, j: (0, 0))],
        out_specs=pl.BlockSpec((tm, D_MODEL), lambda i, j: (i, 0)),
        scratch_shapes=[pltpu.VMEM((tm, D_MODEL), bf16), pltpu.VMEM((tm, D_MODEL), f32)],
        compiler_params=_cparams("parallel", "arbitrary"),
        name=name,
    )(x, w_in, w_in, w_out, g, b)


def _gates_body(x_ref, w_ref, al_ref, dt_ref, gb_ref, gbt_ref):
    ab = _dot(x_ref[...].astype(bf16), w_ref[...])
    a = ab + dt_ref[...]
    softplus = jnp.maximum(a, 0.0) + jnp.log1p(jnp.exp(-jnp.abs(a)))
    g = -jnp.exp(al_ref[...]) * softplus
    beta = _sigmoid(ab)
    tm = ab.shape[0]
    row = lax.broadcasted_iota(jnp.int32, (CHUNK, CHUNK), 0)
    col = lax.broadcasted_iota(jnp.int32, (CHUNK, CHUNK), 1)
    tril = (row >= col).astype(f32)
    gc = jnp.concatenate(
        [jnp.dot(tril, g[c * CHUNK:(c + 1) * CHUNK], precision=lax.Precision.HIGHEST,
                 preferred_element_type=f32) for c in range(tm // CHUNK)], axis=0)
    lane = lax.broadcasted_iota(jnp.int32, (1, LANES), 1)
    gb = jnp.where(lane < GDN_V_HEADS, gc, jnp.where(lane < 2 * GDN_V_HEADS, beta, g))
    gb_ref[...] = gb
    gbt_ref[...] = gb.T


def _gates(x, w_ab, alog, dtb, tm, name):
    m = x.shape[0]
    return pl.pallas_call(
        _gates_body,
        out_shape=(jax.ShapeDtypeStruct((m, LANES), f32), jax.ShapeDtypeStruct((LANES, m), f32)),
        grid=(m // tm,),
        in_specs=[pl.BlockSpec((tm, D_MODEL), lambda i: (i, 0)),
                  pl.BlockSpec((D_MODEL, LANES), lambda i: (0, 0)),
                  pl.BlockSpec((1, LANES), lambda i: (0, 0)),
                  pl.BlockSpec((1, LANES), lambda i: (0, 0))],
        out_specs=(pl.BlockSpec((tm, LANES), lambda i: (i, 0)),
                   pl.BlockSpec((LANES, tm), lambda i: (0, i))),
        compiler_params=_cparams("parallel"),
        name=name,
    )(x, w_ab, alog, dtb)


def _lane_column(x, lane, idx):
    return jnp.sum(jnp.where(lane == idx, x, 0.0), axis=1, keepdims=True)


def _l2norm(x):
    return x * lax.rsqrt(jnp.sum(x * x, axis=-1, keepdims=True) + L2_EPS)


def _gated_rmsnorm(o, nw, z):
    return o * lax.rsqrt(jnp.mean(o * o, axis=-1, keepdims=True) + RMS_EPS) * nw * _silu(z)


def _unit_lower_inverses(neg_ls, eye, row, col):
    def same_block(size):
        return (row // size) == (col // size)

    base = 2 * SUBLANES
    inside = same_block(base)
    powers = [jnp.where(inside, n, 0.0) for n in neg_ls]
    invs = [eye + p for p in powers]
    span = 2
    while span < base:
        pbs = [p.astype(bf16) for p in powers]
        powers = [_dot(pb, pb) for pb in pbs]
        invs = [inv + _dot(inv.astype(bf16), p.astype(bf16)) for inv, p in zip(invs, powers)]
        span *= 2
    size = base
    while size < CHUNK:
        wider = same_block(2 * size)
        offs = [jnp.where(wider & ~inside, n, 0.0).astype(bf16) for n in neg_ls]
        inv_bs = [inv.astype(bf16) for inv in invs]
        mids = [_dot(off, inv_b).astype(bf16) for off, inv_b in zip(offs, inv_bs)]
        invs = [inv + _dot(inv_b, mid) for inv, inv_b, mid in zip(invs, inv_bs, mids)]
        inside = wider
        size *= 2
    return invs


def _gdn_prompt_body(q_ref, k_ref, v_ref, z_ref, gb_ref, gbt_ref, cwq_ref, cwk_ref, cwv_ref, nw_ref,
                     o_ref, st_ref, sb_scr, ob_scr, sk_scr, oq_scr, st_scr, el_scr, s_scr):
    hq = pl.program_id(1)
    nchunk = q_ref.shape[0] // CHUNK
    row = lax.broadcasted_iota(jnp.int32, (CHUNK, CHUNK), 0)
    col = lax.broadcasted_iota(jnp.int32, (CHUNK, CHUNK), 1)
    tri = row >= col
    strict = row > col
    eye = (row == col).astype(f32)
    lane = lax.broadcasted_iota(jnp.int32, (1, LANES), 1)
    rows8 = lax.broadcasted_iota(jnp.int32, (SUBLANES, 1), 0)
    cwq = cwq_ref[...]
    cwk = cwk_ref[...]
    cwv = cwv_ref[...]
    nw = nw_ref[...]

    tile0 = pl.multiple_of((2 * hq) // SUBLANES * SUBLANES, SUBLANES)

    def conv_silu(ref, c, r0, cw):
        x = ref[pl.ds(r0, CHUNK), :]
        p0 = pl.multiple_of(jnp.maximum(r0 - SUBLANES, 0), SUBLANES)
        prev8 = jnp.where(c > 0, ref[pl.ds(p0, SUBLANES), :], 0.0)
        acc = x * cw[GDN_CONV_W - 1:GDN_CONV_W]
        for s in range(1, GDN_CONV_W):
            rolled = pltpu.roll(x, s, 0)
            top = jnp.where(rows8 < s, pltpu.roll(prev8, s, 0), rolled[:SUBLANES])
            shifted = jnp.concatenate([top, rolled[SUBLANES:]], axis=0)
            acc = acc + shifted * cw[GDN_CONV_W - 1 - s:GDN_CONV_W - s]
        return _silu(acc)

    def prepare(i, carry):
        chains = []
        for cc in range(PREP_CHUNKS):
            c = i * PREP_CHUNKS + cc
            r0 = pl.multiple_of(c * CHUNK, CHUNK)
            qn = _l2norm(conv_silu(q_ref, c, r0, cwq)) * GDN_DK ** -0.5
            kn = _l2norm(conv_silu(k_ref, c, r0, cwk))
            vc = conv_silu(v_ref, c, r0, cwv)
            kb = kn.astype(bf16)
            kk = _dot_nt(kb, kb)
            qk = _dot_nt(qn.astype(bf16), kb)
            k_t = kn.T
            gbc = gb_ref[pl.ds(r0, CHUNK), :]
            gbt8 = gbt_ref[pl.ds(tile0, SUBLANES), pl.ds(r0, CHUNK)]
            for hh in range(2):
                hv = 2 * hq + hh
                gc_col = _lane_column(gbc, lane, hv)
                beta_col = _lane_column(gbc, lane, GDN_V_HEADS + hv)
                gc_row = jnp.sum(jnp.where(rows8 == hv % SUBLANES, gbt8, 0.0), axis=0, keepdims=True)
                decay = jnp.where(tri, jnp.exp(jnp.where(tri, gc_col - gc_row, 0.0)), 0.0)
                g_last = gc_row[:, CHUNK - 1:CHUNK]
                el_scr[hh, c] = jnp.broadcast_to(jnp.exp(g_last), (SUBLANES, GDN_DV))
                vh = vc[:, hh * GDN_DV:(hh + 1) * GDN_DV]
                chains.append(dict(
                    hh=hh, c=c,
                    neg_l=jnp.where(strict, -(beta_col * kk * decay), 0.0),
                    vb=(vh * beta_col).astype(bf16),
                    kbg=(kn * (beta_col * jnp.exp(gc_col))).astype(bf16),
                    a=jnp.where(tri, qk * decay, 0.0).astype(bf16),
                    qg=qn * jnp.exp(gc_col),
                    kd=(k_t * jnp.exp(g_last - gc_row)).astype(bf16)))
        invs = _unit_lower_inverses([ch["neg_l"] for ch in chains], eye, row, col)
        inv_bs = [inv.astype(bf16) for inv in invs]
        us = [_dot(inv_b, ch["vb"]).astype(bf16) for ch, inv_b in zip(chains, inv_bs)]
        ws = [_dot(inv_b, ch["kbg"]).astype(bf16) for ch, inv_b in zip(chains, inv_bs)]
        for ch, u, w in zip(chains, us, ws):
            hh, c = ch["hh"], ch["c"]
            sk_scr[hh, c] = _dot(ch["kd"], w).astype(bf16)
            sb_scr[hh, c] = _dot(ch["kd"], u)
            oq_scr[hh, c] = (ch["qg"] - _dot(ch["a"], w)).astype(bf16)
            ob_scr[hh, c] = _dot(ch["a"], u)
        return carry

    def advance(c, carry):
        for hh in range(2):
            state = s_scr[hh]
            state_b = state.astype(bf16)
            st_scr[hh, c] = state_b
            s_scr[hh] = state * el_scr[hh, c][0:1] + (sb_scr[hh, c] - _dot(sk_scr[hh, c], state_b))
        return carry

    def emit(i, carry):
        os_ = []
        for cc in range(OUT_CHUNKS):
            c = i * OUT_CHUNKS + cc
            for hh in range(2):
                os_.append((c, hh, _dot(oq_scr[hh, c], st_scr[hh, c]) + ob_scr[hh, c]))
        for cc in range(OUT_CHUNKS):
            c = i * OUT_CHUNKS + cc
            r0 = pl.multiple_of(c * CHUNK, CHUNK)
            zc = z_ref[pl.ds(r0, CHUNK), :]
            outs = [_gated_rmsnorm(o, nw, zc[:, hh * GDN_DV:(hh + 1) * GDN_DV])
                    for (c2, hh, o) in os_[2 * cc:2 * cc + 2]]
            o_ref[pl.ds(r0, CHUNK), :] = jnp.concatenate(outs, axis=1).astype(o_ref.dtype)
        return carry

    lax.fori_loop(0, nchunk // PREP_CHUNKS, prepare, 0)
    s_scr[...] = jnp.zeros_like(s_scr)
    lax.fori_loop(0, nchunk, advance, 0)
    st_ref[0] = s_scr[...]
    lax.fori_loop(0, nchunk // OUT_CHUNKS, emit, 0)


def _gdn_prompt(p, gb, gbt, conv_w, norm_w, batch, seq, name):
    nqk = GDN_QK_HEADS
    zoff = GDN_CONV_DIM // (2 * GDN_DV)
    nchunk = seq // CHUNK
    tiles_bf = pltpu.VMEM((2, nchunk, CHUNK, CHUNK), bf16)
    tiles_f32 = pltpu.VMEM((2, nchunk, CHUNK, CHUNK), f32)
    return pl.pallas_call(
        _gdn_prompt_body,
        out_shape=(jax.ShapeDtypeStruct((batch * seq, GDN_V_DIM), bf16),
                   jax.ShapeDtypeStruct((batch, GDN_V_HEADS, GDN_DK, GDN_DV), f32)),
        grid=(batch, nqk),
        in_specs=[pl.BlockSpec((seq, GDN_DK), lambda b, h: (b, h)),
                  pl.BlockSpec((seq, GDN_DK), lambda b, h: (b, nqk + h)),
                  pl.BlockSpec((seq, 2 * GDN_DV), lambda b, h: (b, nqk + h)),
                  pl.BlockSpec((seq, 2 * GDN_DV), lambda b, h: (b, zoff + h)),
                  pl.BlockSpec((seq, LANES), lambda b, h: (b, 0)),
                  pl.BlockSpec((LANES, seq), lambda b, h: (0, b)),
                  pl.BlockSpec((GDN_CONV_W, GDN_DK), lambda b, h: (0, h)),
                  pl.BlockSpec((GDN_CONV_W, GDN_DK), lambda b, h: (0, nqk + h)),
                  pl.BlockSpec((GDN_CONV_W, 2 * GDN_DV), lambda b, h: (0, nqk + h)),
                  pl.BlockSpec((1, GDN_DV), lambda b, h: (0, 0))],
        out_specs=(pl.BlockSpec((seq, 2 * GDN_DV), lambda b, h: (b, h)),
                   pl.BlockSpec((1, 2, GDN_DK, GDN_DV), lambda b, h: (b, h, 0, 0))),
        scratch_shapes=[tiles_f32, tiles_f32, tiles_bf, tiles_bf, tiles_bf,
                        pltpu.VMEM((2, nchunk, SUBLANES, GDN_DV), f32),
                        pltpu.VMEM((2, GDN_DK, GDN_DV), f32)],
        compiler_params=_cparams("parallel", "parallel"),
        name=name,
    )(p, p, p, p, gb, gbt, conv_w, conv_w, conv_w, norm_w)


def _gdn_sample_body(q_ref, k_ref, v_ref, z_ref, aq_ref, ak_ref, av_ref, gb_ref, cwq_ref, cwk_ref, cwv_ref,
                     nw_ref, st_ref, *rest):
    o_ref, sto_ref, q_scr, k_scr, v_scr, o_scr = rest[-6:]
    hq = pl.program_id(1)
    rows = q_ref.shape[0]
    nreq = rows // SAMPLE_ROWS
    tok = lax.broadcasted_iota(jnp.int32, (rows, 1), 0) % SAMPLE_ROWS
    lane = lax.broadcasted_iota(jnp.int32, (1, LANES), 1)
    rows8 = lax.broadcasted_iota(jnp.int32, (SAMPLE_ROWS, 1), 0)
    nw = nw_ref[...]
    hist = GDN_CONV_W - 1

    def conv_silu(a, x, cw):
        acc = jnp.zeros_like(x)
        for w in range(GDN_CONV_W):
            ra = a if w == 0 else pltpu.roll(a, rows - w, 0)
            rx = x if w == hist else pltpu.roll(x, hist - w, 0)
            acc = acc + jnp.where(tok + w < hist, ra, rx) * cw[w:w + 1]
        return _silu(acc)

    q_scr[...] = _l2norm(conv_silu(aq_ref[...], q_ref[...], cwq_ref[...])) * GDN_DK ** -0.5
    k_scr[...] = _l2norm(conv_silu(ak_ref[...], k_ref[...], cwk_ref[...]))
    v_scr[...] = conv_silu(av_ref[...], v_ref[...], cwv_ref[...])
    pad = jnp.zeros((LANES - 2 * SAMPLE_ROWS, GDN_DK), f32)

    def requests(i, carry):
        work = []
        for rr in range(REQ_LOCK):
            bi = i * REQ_LOCK + rr
            r0 = pl.multiple_of(bi * SAMPLE_ROWS, SAMPLE_ROWS)
            k8 = k_scr[pl.ds(r0, SAMPLE_ROWS), :]
            q8 = q_scr[pl.ds(r0, SAMPLE_ROWS), :]
            v8 = v_scr[pl.ds(r0, SAMPLE_ROWS), :]
            gb8 = gb_ref[pl.ds(r0, SAMPLE_ROWS), :]
            kq_t = jnp.concatenate([k8, q8, pad], axis=0).T
            for hh in range(2):
                hv = 2 * hq + hh
                work.append(dict(bi=bi, hh=hh, r0=r0, kq_t=kq_t, v=v8[:, hh * GDN_DV:(hh + 1) * GDN_DV],
                                 g=_lane_column(gb8, lane, 2 * GDN_V_HEADS + hv),
                                 beta=_lane_column(gb8, lane, GDN_V_HEADS + hv),
                                 state=st_ref[bi, hh], o=jnp.zeros((SAMPLE_ROWS, GDN_DV), f32)))
        for t in range(GDN_CONV_W):
            for w in work:
                w["state"] = w["state"] * jnp.exp(w["g"][t:t + 1])
            for w in work:
                k_col = w["kq_t"][:, t:t + 1]
                ks = jnp.sum(w["state"] * k_col, axis=0, keepdims=True)
                delta = (w["v"][t:t + 1] - ks) * w["beta"][t:t + 1]
                w["state"] = w["state"] + k_col * delta
            for w in work:
                q_col = w["kq_t"][:, SAMPLE_ROWS + t:SAMPLE_ROWS + t + 1]
                o_t = jnp.sum(w["state"] * q_col, axis=0, keepdims=True)
                w["o"] = jnp.where(rows8 == t, o_t, w["o"])
        for w in work:
            sto_ref[w["bi"], w["hh"]] = w["state"]
        for w0, w1 in zip(work[0::2], work[1::2]):
            z8 = z_ref[pl.ds(w0["r0"], SAMPLE_ROWS), :]
            o_scr[pl.ds(w0["r0"], SAMPLE_ROWS), :] = jnp.concatenate(
                [_gated_rmsnorm(w["o"], nw, z8[:, w["hh"] * GDN_DV:(w["hh"] + 1) * GDN_DV]) for w in (w0, w1)], axis=1)
        return carry

    lax.fori_loop(0, nreq // REQ_LOCK, requests, 0)
    o_ref[...] = o_scr[...].astype(o_ref.dtype)


def _gdn_sample(p, conv_buf, gb, conv_w, norm_w, state, new_state, layer, nreq_blk, name):
    assert GDN_CONV_W == 4 and SAMPLE_ROWS == 8
    nqk = GDN_QK_HEADS
    zoff = GDN_CONV_DIM // (2 * GDN_DV)
    nreq = state.shape[1]
    rows = nreq_blk * SAMPLE_ROWS
    st_spec = pl.BlockSpec((None, nreq_blk, 2, GDN_DK, GDN_DV), lambda i, h: (layer, i, h, 0, 0))
    carried = [] if new_state is None else [new_state]
    return pl.pallas_call(
        _gdn_sample_body,
        out_shape=(jax.ShapeDtypeStruct((nreq * SAMPLE_ROWS, GDN_V_DIM), bf16),
                   jax.ShapeDtypeStruct(state.shape, f32)),
        grid=(nreq // nreq_blk, nqk),
        in_specs=[pl.BlockSpec((rows, GDN_DK), lambda i, h: (i, h)),
                  pl.BlockSpec((rows, GDN_DK), lambda i, h: (i, nqk + h)),
                  pl.BlockSpec((rows, 2 * GDN_DV), lambda i, h: (i, nqk + h)),
                  pl.BlockSpec((rows, 2 * GDN_DV), lambda i, h: (i, zoff + h)),
                  pl.BlockSpec((rows, GDN_DK), lambda i, h: (i, h)),
                  pl.BlockSpec((rows, GDN_DK), lambda i, h: (i, nqk + h)),
                  pl.BlockSpec((rows, 2 * GDN_DV), lambda i, h: (i, nqk + h)),
                  pl.BlockSpec((rows, LANES), lambda i, h: (i, 0)),
                  pl.BlockSpec((GDN_CONV_W, GDN_DK), lambda i, h: (0, h)),
                  pl.BlockSpec((GDN_CONV_W, GDN_DK), lambda i, h: (0, nqk + h)),
                  pl.BlockSpec((GDN_CONV_W, 2 * GDN_DV), lambda i, h: (0, nqk + h)),
                  pl.BlockSpec((1, GDN_DV), lambda i, h: (0, 0)),
                  st_spec] + [pl.BlockSpec(memory_space=pl.ANY)] * len(carried),
        out_specs=(pl.BlockSpec((rows, 2 * GDN_DV), lambda i, h: (i, h)), st_spec),
        scratch_shapes=[pltpu.VMEM((rows, GDN_DK), f32), pltpu.VMEM((rows, GDN_DK), f32),
                        pltpu.VMEM((rows, 2 * GDN_DV), f32), pltpu.VMEM((rows, 2 * GDN_DV), f32)],
        input_output_aliases={13: 1} if carried else {},
        compiler_params=_cparams("parallel", "parallel"),
        name=name,
    )(p, p, p, p, conv_buf, conv_buf, conv_buf, gb, conv_w, conv_w, conv_w, norm_w, state, *carried)


def _swa_prompt_body(*refs):
    qkv_refs = refs[:3 * SWA_GROUPS]
    bias_ref, o_ref, o_scr, l_scr = refs[3 * SWA_GROUPS:]
    seq = o_ref.shape[0]
    lane = lax.broadcasted_iota(jnp.int32, (1, LANES), 1)
    head0 = lane < SWA_HEAD_DIM
    blk = SWA_KEYS
    for g, d in enumerate(SWA_DILATIONS):
        q_ref, k_ref, v_ref = qkv_refs[3 * g:3 * g + 3]
        nb = seq // (d * blk)

        def rows_of(start, d=d):
            return pl.ds(start, blk) if d == 1 else pl.ds(start, blk, stride=d)

        def blocks(i, carry, g=g, d=d, nb=nb, q_ref=q_ref, k_ref=k_ref, v_ref=v_ref, rows_of=rows_of):
            has_prev = nb > 1
            work = []
            for bb in range(ATT_BLOCKS):
                idx = i * ATT_BLOCKS + bb
                r = idx // nb
                b = idx % nb
                cur = rows_of(r + d * blk * b)
                q = q_ref[cur, :] * SWA_HEAD_DIM ** -0.5
                kc = k_ref[cur, :].astype(bf16)
                vc = v_ref[cur, :].astype(bf16)
                kp = vp = no_prev = None
                if has_prev:
                    prev = rows_of(r + d * blk * jnp.maximum(b - 1, 0))
                    no_prev = jnp.where(b == 0, -jnp.inf, 0.0).astype(f32)
                    kp = k_ref[prev, :].astype(bf16)
                    vp = v_ref[prev, :].astype(bf16)
                for hh in range(2):
                    qh = jnp.where(head0 if hh == 0 else ~head0, q, 0.0).astype(bf16)
                    work.append(dict(cur=cur, hh=hh, qh=qh, kc=kc, vc=vc, kp=kp, vp=vp, no_prev=no_prev))
            for w in work:
                w["s_c"] = _dot_nt(w["qh"], w["kc"]) + bias_ref[g, w["hh"]][:, blk:]
            if has_prev:
                for w in work:
                    w["s_p"] = _dot_nt(w["qh"], w["kp"]) + (bias_ref[g, w["hh"]][:, :blk] + w["no_prev"])
            for w in work:
                m = jnp.max(jnp.maximum(w["s_c"], w["s_p"]) if has_prev else w["s_c"], axis=-1, keepdims=True)
                p_c = jnp.exp(w["s_c"] - m)
                w["p_c"] = p_c.astype(bf16)
                if has_prev:
                    p_p = jnp.exp(w["s_p"] - m)
                    w["p_p"] = p_p.astype(bf16)
                    p_c = p_c + p_p
                w["m"], w["den"] = m, jnp.sum(p_c, axis=-1, keepdims=True)
            for w in work:
                w["pv"] = _dot(w["p_c"], w["vc"])
            if has_prev:
                for w in work:
                    w["pv"] = w["pv"] + _dot(w["p_p"], w["vp"])
            for w0, w1 in zip(work[0::2], work[1::2]):
                o_scr[g, w0["cur"], :] = jnp.where(head0, w0["pv"] / w0["den"], w1["pv"] / w1["den"])
                l_scr[g, w0["cur"], :] = jnp.where(head0, w0["m"] + jnp.log(w0["den"]), w1["m"] + jnp.log(w1["den"]))
            return carry

        lax.fori_loop(0, seq // (blk * ATT_BLOCKS), blocks, 0)

    def merge(c, carry):
        rr = pl.ds(pl.multiple_of(c * blk, blk), blk)
        ls = [l_scr[g, rr, :] for g in range(SWA_GROUPS)]
        mx = jnp.maximum(jnp.maximum(ls[0], ls[1]), ls[2])
        ws = [jnp.exp(l - mx) for l in ls]
        num = ws[0] * o_scr[0, rr, :] + ws[1] * o_scr[1, rr, :] + ws[2] * o_scr[2, rr, :]
        o_ref[rr, :] = (num / (ws[0] + ws[1] + ws[2])).astype(o_ref.dtype)
        return carry

    lax.fori_loop(0, seq // blk, merge, 0)


def _swa_prompt(p, bias_t, batch, seq, name):
    npair = SWA_INNER // LANES
    in_specs = []
    for g in range(SWA_GROUPS):
        for c in range(3):
            off = (g * 3 + c) * npair
            in_specs.append(pl.BlockSpec((seq, LANES), lambda b, h, off=off: (b, off + h)))
    in_specs.append(pl.BlockSpec((SWA_GROUPS, 2, SWA_KEYS, 2 * SWA_KEYS), lambda b, h: (0, h, 0, 0)))
    return pl.pallas_call(
        _swa_prompt_body,
        out_shape=jax.ShapeDtypeStruct((batch * seq, SWA_INNER), bf16),
        grid=(batch, npair),
        in_specs=in_specs,
        out_specs=pl.BlockSpec((seq, LANES), lambda b, h: (b, h)),
        scratch_shapes=[pltpu.VMEM((SWA_GROUPS, seq, LANES), f32), pltpu.VMEM((SWA_GROUPS, seq, LANES), f32)],
        compiler_params=_cparams("parallel", "parallel"),
        name=name,
    )(*([p] * (3 * SWA_GROUPS)), bias_t)


def _swa_sample_body(*refs):
    qkv_refs = refs[:3 * SWA_GROUPS]
    cache_refs = refs[3 * SWA_GROUPS:4 * SWA_GROUPS]
    tb_refs = refs[4 * SWA_GROUPS:5 * SWA_GROUPS]
    tn_ref, o_ref = refs[5 * SWA_GROUPS:]
    nhead = cache_refs[0].shape[1]
    heads = [slice(h * SAMPLE_ROWS, (h + 1) * SAMPLE_ROWS) for h in range(nhead)]

    def by_head(ref):
        x = ref[...]
        return jnp.concatenate([x[:, h * SWA_HEAD_DIM:(h + 1) * SWA_HEAD_DIM] for h in range(nhead)], axis=0)

    qs, s_cache, s_new = [], [], []
    for g in range(SWA_GROUPS):
        q = (by_head(qkv_refs[3 * g]) * SWA_HEAD_DIM ** -0.5).astype(bf16)
        qs.append(q)
        s_cache.append(jnp.concatenate(
            [_dot(q[heads[h]], cache_refs[g][0, h].astype(bf16)) for h in range(nhead)], axis=0) + tb_refs[g][...])
    for g in range(SWA_GROUPS):
        k_new = by_head(qkv_refs[3 * g + 1]).astype(bf16)
        s_new.append(_dot_nt(qs[g], k_new) + tn_ref[g])
    ms, dens, ps, pns = [], [], [], []
    for g in range(SWA_GROUPS):
        m = jnp.maximum(jnp.max(s_cache[g], axis=-1, keepdims=True), jnp.max(s_new[g], axis=-1, keepdims=True))
        p = jnp.exp(s_cache[g] - m)
        pn = jnp.exp(s_new[g] - m)
        ms.append(m)
        dens.append(jnp.sum(p, axis=-1, keepdims=True) + jnp.sum(pn, axis=-1, keepdims=True))
        ps.append(p.astype(bf16))
        pns.append(pn.astype(bf16))
    pvs = []
    for g in range(SWA_GROUPS):
        pvs.append(jnp.concatenate(
            [_dot_nt(ps[g][heads[h]], cache_refs[g][1, h].astype(bf16)) for h in range(nhead)], axis=0))
    for g in range(SWA_GROUPS):
        v_new = by_head(qkv_refs[3 * g + 2]).astype(bf16)
        pvs[g] = pvs[g] + _dot(pns[g], v_new)
    o_g = [pv / den for pv, den in zip(pvs, dens)]
    l_g = [m + jnp.log(den) for m, den in zip(ms, dens)]
    mx = jnp.maximum(jnp.maximum(l_g[0], l_g[1]), l_g[2])
    ws = [jnp.exp(l - mx) for l in l_g]
    merged = (ws[0] * o_g[0] + ws[1] * o_g[1] + ws[2] * o_g[2]) / (ws[0] + ws[1] + ws[2])
    o_ref[...] = jnp.concatenate([merged[heads[h]] for h in range(nhead)], axis=1)


def _swa_sample(p, caches_t, layer, tb, tn, name):
    nreq = p.shape[0] // SAMPLE_ROWS
    nhalf = 2
    nhead = SWA_HEADS // nhalf
    nrow = nhead * SAMPLE_ROWS
    width = nhead * SWA_HEAD_DIM
    in_specs = [pl.BlockSpec((SAMPLE_ROWS, width), lambda h, b, blk=blk: (b, nhalf * blk + h))
                for blk in range(3 * SWA_GROUPS)]
    for c in caches_t:
        in_specs.append(pl.BlockSpec((None, None, 2, nhead, SWA_HEAD_DIM, c.shape[-1]),
                                     lambda h, b: (layer, b, 0, h, 0, 0)))
    for t in tb:
        in_specs.append(pl.BlockSpec((nrow, t.shape[-1]), lambda h, b: (h, 0)))
    in_specs.append(pl.BlockSpec((None, SWA_GROUPS, nrow, nrow), lambda h, b: (h, 0, 0, 0)))
    return pl.pallas_call(
        _swa_sample_body,
        out_shape=jax.ShapeDtypeStruct((nreq * SAMPLE_ROWS, SWA_INNER), f32),
        grid=(nhalf, nreq),
        in_specs=in_specs,
        out_specs=pl.BlockSpec((SAMPLE_ROWS, width), lambda h, b: (b, h)),
        compiler_params=_cparams("parallel", "parallel"),
        name=name,
    )(*([p] * (3 * SWA_GROUPS)), *caches_t, *tb, tn)


def _window_t_body(p_ref, *rest):
    o_ref = rest[-1]
    o_ref[...] = p_ref[...].T


def _window_t(p, stacked, layer, nlayer, g, batch, seq, keep, name):
    tk = min(keep, 512)
    first = (seq - keep) // tk
    per_req = seq // tk
    col0 = 3 * g + 1
    carried = [] if stacked is None else [stacked]
    return pl.pallas_call(
        _window_t_body,
        out_shape=jax.ShapeDtypeStruct((nlayer, batch, 2, SWA_INNER, keep), f32),
        grid=(batch, 2, keep // tk),
        in_specs=[pl.BlockSpec((tk, SWA_INNER), lambda b, c, i: (b * per_req + first + i, col0 + c))]
        + [pl.BlockSpec(memory_space=pl.ANY)] * len(carried),
        out_specs=pl.BlockSpec((None, None, None, SWA_INNER, tk), lambda b, c, i: (layer, b, c, 0, i)),
        input_output_aliases={1: 0} if carried else {},
        compiler_params=_cparams("parallel", "parallel", "parallel"),
        name=name,
    )(p, *carried)


def _t5_bucket(dist):
    max_exact = N_BUCKETS // 2
    dd = jnp.maximum(dist, 1).astype(f32)
    large = max_exact + (jnp.log(dd / max_exact) / math.log(MAX_DISTANCE / max_exact)
                         * (N_BUCKETS - max_exact)).astype(jnp.int32)
    return jnp.where(dist < max_exact, dist, jnp.minimum(large, N_BUCKETS - 1))


def _select_rows(table, idx):
    onehot = (idx[..., None] == jnp.arange(table.shape[0])).astype(f32)
    return jnp.einsum("...n,nc->...c", onehot, table, precision=lax.Precision.HIGHEST)


def _bias_tables(rel_bias, ntok, cache_lens):
    neg = -jnp.inf
    i = jnp.arange(SWA_KEYS)[:, None]
    jj = jnp.arange(2 * SWA_KEYS)[None, :]
    steps = SWA_KEYS + i - jj
    band = (steps >= 0) & (steps <= SWA_KEYS)
    t_row = jnp.where(jnp.arange(SAMPLE_ROWS) < ntok, jnp.arange(SAMPLE_ROWS), 0)[:, None]
    t_new = jnp.arange(SAMPLE_ROWS)[None, :]
    same_head = jnp.eye(SWA_HEADS, dtype=bool)[:, None, :, None]
    nhalf = SWA_HEADS // 2 * SAMPLE_ROWS
    prompt_t, cached_t, new_t = [], [], []
    for g, d in enumerate(SWA_DILATIONS):
        dist = jnp.arange(SWA_KEYS + 1, dtype=jnp.int32) * d
        bias_g = _select_rows(rel_bias[:, g * SWA_HEADS:(g + 1) * SWA_HEADS], _t5_bucket(dist))
        toeplitz = _select_rows(bias_g, jnp.clip(steps, 0, SWA_KEYS))
        prompt_t.append(jnp.where(band[None], jnp.transpose(toeplitz, (2, 0, 1)), neg))
        length = cache_lens[g]
        back = length + t_row - jnp.arange(length)[None, :]
        ok = (back % d == 0) & (back >= d) & (back <= SWA_KEYS * d)
        vals = _select_rows(bias_g, jnp.clip(back // d, 0, SWA_KEYS))
        cached = jnp.where(ok[None], jnp.transpose(vals, (2, 0, 1)), neg)
        cached_t.append(cached.reshape(SWA_HEADS * SAMPLE_ROWS, length))
        back = t_row - t_new
        ok = (back % d == 0) & (back >= 0) & (t_new < ntok)
        vals = _select_rows(bias_g, jnp.clip(back // d, 0, SWA_KEYS))
        fresh = jnp.where(ok[None], jnp.transpose(vals, (2, 0, 1)), neg)
        full = jnp.where(same_head, fresh[:, :, None, :], neg).reshape(2 * nhalf, 2 * nhalf)
        new_t.append(jnp.stack([full[:nhalf, :nhalf], full[nhalf:, nhalf:]]))
    return jnp.stack(prompt_t), cached_t, jnp.stack(new_t, axis=1)


def kernel(x_prompt, x_sample, state_gdn, state_conv, cache_win0, cache_win1, cache_win2, w_in_gdn, conv_w_gdn,
           a_log_gdn, dt_bias_gdn, norm_gdn, w_out_gdn, w_in_swa, w_out_swa, rel_bias, w_ffn_in, w_ffn_out,
           ln_mix_g, ln_mix_b, ln_ffn_g, ln_ffn_b):
    batch, seq, _ = x_prompt.shape
    nreq, ntok, _ = x_sample.shape
    caches = (cache_win0, cache_win1, cache_win2)
    mp = batch * seq
    tm_p = 1024
    tm_s = nreq * SAMPLE_ROWS

    xp = x_prompt.reshape(mp, D_MODEL)
    xs = jnp.pad(x_sample, ((0, 0), (0, SAMPLE_ROWS - ntok), (0, 0))).reshape(tm_s, D_MODEL)
    caches_t = tuple(jnp.transpose(c, (0, 1, 3, 4, 5, 2)) for c in caches)
    bias_p, bias_s, bias_new = _bias_tables(rel_bias, ntok, [c.shape[2] for c in caches])

    gdn_p, conv_p, conv_s = [], [], []
    gdn_s = None
    win_p = [None] * SWA_GROUPS
    win_s = [[] for _ in range(SWA_GROUPS)]
    for i in range(DEPTH):
        j = i // N_MIXERS
        ln_g, ln_b = ln_mix_g[i][None], ln_mix_b[i][None]
        if i % N_MIXERS == 0:
            w = w_in_gdn[j]
            w_qkvz = w[:, :GDN_QKVZ_DIM].astype(bf16)
            w_a = w[:, GDN_QKVZ_DIM:GDN_QKVZ_DIM + GDN_V_HEADS]
            w_b = w[:, GDN_QKVZ_DIM + GDN_V_HEADS:]
            fill = jnp.zeros((D_MODEL, LANES - 3 * GDN_V_HEADS), f32)
            w_ab = jnp.concatenate([w_a, w_b, w_a, fill], axis=1).astype(bf16)
            zpad = jnp.zeros((LANES - 3 * GDN_V_HEADS,), f32)
            zh = jnp.zeros((GDN_V_HEADS,), f32)
            alog = jnp.concatenate([a_log_gdn[j], zh, a_log_gdn[j], zpad])[None]
            dtb = jnp.concatenate([dt_bias_gdn[j], zh, dt_bias_gdn[j], zpad])[None]
            conv_w = conv_w_gdn[j]
            norm_w = norm_gdn[j][None]
            w_out = w_out_gdn[j].astype(bf16)

            pp = _proj(xp, w_qkvz, tm_p, 1024, f"gdn_in_p{j}")
            ps = _proj(xs, w_qkvz, tm_s, 1024, f"gdn_in_s{j}")
            gb_p, gbt_p = _gates(xp, w_ab, alog, dtb, tm_p, f"gdn_gates_p{j}")
            gb_s, _ = _gates(xs, w_ab, alog, dtb, tm_s, f"gdn_gates_s{j}")
            o_p, st_p = _gdn_prompt(pp, gb_p, gbt_p, conv_w, norm_w, batch, seq, f"gdn_prompt{j}")
            cbuf = jnp.pad(state_conv[j], ((0, 0), (0, SAMPLE_ROWS - (GDN_CONV_W - 1)), (0, 0)))
            o_s, gdn_s = _gdn_sample(ps, cbuf.reshape(tm_s, GDN_CONV_DIM), gb_s, conv_w, norm_w, state_gdn, gdn_s,
                                     j, 8, f"gdn_sample{j}")
            xp = _mm_ln(o_p, w_out, xp, ln_g, ln_b, tm_p, f"gdn_out_p{j}")
            xs = _mm_ln(o_s, w_out, xs, ln_g, ln_b, tm_s, f"gdn_out_s{j}")
            gdn_p.append(st_p)
            conv_p.append(pp.reshape(batch, seq, GDN_QKVZ_DIM)[:, seq - (GDN_CONV_W - 1):, :GDN_CONV_DIM])
            conv_s.append(ps.reshape(nreq, SAMPLE_ROWS, GDN_QKVZ_DIM)[:, ntok - (GDN_CONV_W - 1):ntok, :GDN_CONV_DIM])
        else:
            w_in = w_in_swa[j].astype(bf16)
            w_out = w_out_swa[j].astype(bf16)
            pp = _proj(xp, w_in, tm_p, 1024, f"swa_in_p{j}")
            ps = _proj(xs, w_in, tm_s, 1024, f"swa_in_s{j}")
            a_p = _swa_prompt(pp, bias_p, batch, seq, f"swa_prompt{j}")
            a_s = _swa_sample(ps, caches_t, j, bias_s, bias_new, f"swa_sample{j}")
            ps6 = ps.reshape(nreq, SAMPLE_ROWS, SWA_GROUPS, 3, SWA_HEADS, SWA_HEAD_DIM)
            xp = _mm_ln(a_p, w_out, xp, ln_g, ln_b, tm_p, f"swa_out_p{j}")
            xs = _mm_ln(a_s, w_out, xs, ln_g, ln_b, tm_s, f"swa_out_s{j}")
            for g in range(SWA_GROUPS):
                keep = min(SWA_KEYS * SWA_DILATIONS[g], seq)
                win_p[g] = _window_t(pp, win_p[g], j, DEPTH // N_MIXERS, g, batch, seq, keep, f"win_p{g}_{j}")
                win_s[g].append(ps6[:, :ntok, g, 1:])
        w_fi = w_ffn_in[i].astype(bf16)
        w_fo = w_ffn_out[i].astype(bf16)
        xp = _ffn(xp, w_fi, w_fo, ln_ffn_g[i][None], ln_ffn_b[i][None], tm_p, f"ffn_p{i}")
        xs = _ffn(xs, w_fi, w_fo, ln_ffn_g[i][None], ln_ffn_b[i][None], tm_s, f"ffn_s{i}")

    y_p = xp.reshape(batch, seq, D_MODEL)
    y_s = xs.reshape(nreq, SAMPLE_ROWS, D_MODEL)[:, :ntok]
    return (y_p, y_s, jnp.stack(gdn_p), jnp.stack(conv_p),
            *[jnp.transpose(w.reshape(w.shape[:3] + (SWA_HEADS, SWA_HEAD_DIM, w.shape[-1])), (0, 1, 5, 2, 3, 4))
              for w in win_p],
            gdn_s, jnp.stack(conv_s),
            jnp.stack(win_s[0]), jnp.stack(win_s[1]), jnp.stack(win_s[2]))
```

```python
import functools
import math

import jax
import jax.numpy as jnp
from jax import lax
from jax.experimental import pallas as pl
from jax.experimental.pallas import tpu as pltpu

f32 = jnp.float32
bf16 = jnp.bfloat16

D_MODEL = 1024
DEPTH = 4
N_MIXERS = 2
GDN_QK_HEADS = 8
GDN_V_HEADS = 16
GDN_DK = 128
GDN_DV = 128
GDN_CONV_W = 4
GDN_QK_DIM = GDN_QK_HEADS * GDN_DK
GDN_V_DIM = GDN_V_HEADS * GDN_DV
GDN_CONV_DIM = 2 * GDN_QK_DIM + GDN_V_DIM
GDN_QKVZ_DIM = GDN_CONV_DIM + GDN_V_DIM
SWA_DILATIONS = (1, 4, 16)
SWA_GROUPS = 3
SWA_HEADS = 16
SWA_HEAD_DIM = 64
SWA_KEYS = 128
SWA_INNER = SWA_HEADS * SWA_HEAD_DIM
SWA_IN_DIM = SWA_GROUPS * 3 * SWA_INNER
N_BUCKETS = 32
MAX_DISTANCE = 2048
D_FF = 2816
DEEPNORM_ALPHA = (2 * DEPTH) ** 0.25
LN_EPS = 1e-5
RMS_EPS = 1e-6
L2_EPS = 1e-6

LANES = 128
SUBLANES = 8
VMEM_LIMIT = 56 * 1024 * 1024

CHUNK = 128
PREP_CHUNKS = 8
OUT_CHUNKS = 4
ATT_BLOCKS = 8
REQ_LOCK = 4
SAMPLE_ROWS = 8
FFN_TILE = 256
FFN_ROWS = 256


def _cparams(*sem):
    return pltpu.CompilerParams(dimension_semantics=sem, vmem_limit_bytes=VMEM_LIMIT)


def _dot(a, b):
    return jnp.dot(a, b, preferred_element_type=f32)


def _dot_nt(a, b):
    return lax.dot_general(a, b, (((1,), (1,)), ((), ())), preferred_element_type=f32)


def _sigmoid(x):
    return 1.0 / (1.0 + jnp.exp(-x))


def _silu(x):
    return x * _sigmoid(x)


def _layer_norm(y, g, b):
    mu = jnp.mean(y, axis=-1, keepdims=True)
    yc = y - mu
    var = jnp.mean(yc * yc, axis=-1, keepdims=True)
    return yc * lax.rsqrt(var + LN_EPS) * g + b


def _proj_body(x_ref, w_ref, o_ref, xb_ref):
    @pl.when(pl.program_id(1) == 0)
    def _():
        xb_ref[...] = x_ref[...].astype(bf16)

    o_ref[...] = _dot(xb_ref[...], w_ref[...]).astype(o_ref.dtype)


def _proj(x, w, tm, tn, name):
    m, k = x.shape
    n = w.shape[1]
    return pl.pallas_call(
        _proj_body,
        out_shape=jax.ShapeDtypeStruct((m, n), f32),
        grid=(m // tm, n // tn),
        in_specs=[pl.BlockSpec((tm, k), lambda i, j: (i, 0)),
                  pl.BlockSpec((k, tn), lambda i, j: (0, j))],
        out_specs=pl.BlockSpec((tm, tn), lambda i, j: (i, j)),
        scratch_shapes=[pltpu.VMEM((tm, k), bf16)],
        compiler_params=_cparams("parallel", "arbitrary"),
        name=name,
    )(x, w)


def _mm_ln_body(a_ref, w_ref, x_ref, g_ref, b_ref, o_ref):
    h = _dot(a_ref[...].astype(bf16), w_ref[...])
    o_ref[...] = _layer_norm(DEEPNORM_ALPHA * x_ref[...] + h, g_ref[...], b_ref[...])


def _mm_ln(a, w, x, g, b, tm, name):
    m, k = a.shape
    return pl.pallas_call(
        _mm_ln_body,
        out_shape=jax.ShapeDtypeStruct((m, D_MODEL), f32),
        grid=(m // tm,),
        in_specs=[pl.BlockSpec((tm, k), lambda i: (i, 0)),
                  pl.BlockSpec((k, D_MODEL), lambda i: (0, 0)),
                  pl.BlockSpec((tm, D_MODEL), lambda i: (i, 0)),
                  pl.BlockSpec((1, D_MODEL), lambda i: (0, 0)),
                  pl.BlockSpec((1, D_MODEL), lambda i: (0, 0))],
        out_specs=pl.BlockSpec((tm, D_MODEL), lambda i: (i, 0)),
        compiler_params=_cparams("parallel"),
        name=name,
    )(a, w, x, g, b)


def _ffn_body(x_ref, wi_ref, wo_ref, g_ref, b_ref, o_ref):
    nf = D_FF // FFN_TILE
    nrow = min(FFN_ROWS, x_ref.shape[0])
    assert x_ref.shape[0] % nrow == 0

    def rows(r, carry):
        rr = pl.ds(pl.multiple_of(r * nrow, nrow), nrow)
        x = x_ref[rr, :]
        xb = x.astype(bf16)
        acts = []
        for j in range(nf):
            h1 = _dot(xb, wi_ref[:, j * FFN_TILE:(j + 1) * FFN_TILE])
            h2 = _dot(xb, wi_ref[:, D_FF + j * FFN_TILE:D_FF + (j + 1) * FFN_TILE])
            acts.append((_silu(h1) * h2).astype(bf16))
        h = _dot(jnp.concatenate(acts, axis=1), wo_ref[...])
        o_ref[rr, :] = _layer_norm(DEEPNORM_ALPHA * x + h, g_ref[...], b_ref[...])
        return carry

    lax.fori_loop(0, x_ref.shape[0] // nrow, rows, 0)


def _ffn(x, w_in, w_out, g, b, tm, name):
    m = x.shape[0]
    once = dict(pipeline_mode=pl.Buffered(1))
    return pl.pallas_call(
        _ffn_body,
        out_shape=jax.ShapeDtypeStruct((m, D_MODEL), f32),
        grid=(m // tm,),
        in_specs=[pl.BlockSpec((tm, D_MODEL), lambda i: (i, 0)),
                  pl.BlockSpec((D_MODEL, 2 * D_FF), lambda i: (0, 0), **once),
                  pl.BlockSpec((D_FF, D_MODEL), lambda i: (0, 0), **once),
                  pl.BlockSpec((1, D_MODEL), lambda i: (0, 0)),
                  pl.BlockSpec((1, D_MODEL), lambda i: (0, 0))],
        out_specs=pl.BlockSpec((tm, D_MODEL), lambda i: (i, 0)),
        compiler_params=_cparams("parallel"),
        name=name,
    )(x, w_in, w_out, g, b)


def _gates_body(x_ref, w_ref, al_ref, dt_ref, gb_ref, gbt_ref):
    ab = _dot(x_ref[...].astype(bf16), w_ref[...])
    a = ab + dt_ref[...]
    softplus = jnp.maximum(a, 0.0) + jnp.log1p(jnp.exp(-jnp.abs(a)))
    g = -jnp.exp(al_ref[...]) * softplus
    beta = _sigmoid(ab)
    tm = ab.shape[0]
    row = lax.broadcasted_iota(jnp.int32, (CHUNK, CHUNK), 0)
    col = lax.broadcasted_iota(jnp.int32, (CHUNK, CHUNK), 1)
    tril = (row >= col).astype(f32)
    gc = jnp.concatenate(
        [jnp.dot(tril, g[c * CHUNK:(c + 1) * CHUNK], precision=lax.Precision.HIGHEST,
                 preferred_element_type=f32) for c in range(tm // CHUNK)], axis=0)
    lane = lax.broadcasted_iota(jnp.int32, (1, LANES), 1)
    gb = jnp.where(lane < GDN_V_HEADS, gc, jnp.where(lane < 2 * GDN_V_HEADS, beta, g))
    gb_ref[...] = gb
    gbt_ref[...] = gb.T


def _gates(x, w_ab, alog, dtb, tm, name):
    m = x.shape[0]
    return pl.pallas_call(
        _gates_body,
        out_shape=(jax.ShapeDtypeStruct((m, LANES), f32), jax.ShapeDtypeStruct((LANES, m), f32)),
        grid=(m // tm,),
        in_specs=[pl.BlockSpec((tm, D_MODEL), lambda i: (i, 0)),
                  pl.BlockSpec((D_MODEL, LANES), lambda i: (0, 0)),
                  pl.BlockSpec((1, LANES), lambda i: (0, 0)),
                  pl.BlockSpec((1, LANES), lambda i: (0, 0))],
        out_specs=(pl.BlockSpec((tm, LANES), lambda i: (i, 0)),
                   pl.BlockSpec((LANES, tm), lambda i: (0, i))),
        compiler_params=_cparams("parallel"),
        name=name,
    )(x, w_ab, alog, dtb)


def _lane_column(x, lane, idx):
    return jnp.sum(jnp.where(lane == idx, x, 0.0), axis=1, keepdims=True)


def _l2norm(x):
    return x * lax.rsqrt(jnp.sum(x * x, axis=-1, keepdims=True) + L2_EPS)


def _gated_rmsnorm(o, nw, z):
    return o * lax.rsqrt(jnp.mean(o * o, axis=-1, keepdims=True) + RMS_EPS) * nw * _silu(z)


def _unit_lower_inverses(neg_ls, eye, row, col):
    def same_block(size):
        return (row // size) == (col // size)

    base = 2 * SUBLANES
    inside = same_block(base)
    powers = [jnp.where(inside, n, 0.0) for n in neg_ls]
    invs = [eye + p for p in powers]
    span = 2
    while span < base:
        pbs = [p.astype(bf16) for p in powers]
        powers = [_dot(pb, pb) for pb in pbs]
        invs = [inv + _dot(inv.astype(bf16), p.astype(bf16)) for inv, p in zip(invs, powers)]
        span *= 2
    size = base
    while size < CHUNK:
        wider = same_block(2 * size)
        offs = [jnp.where(wider & ~inside, n, 0.0).astype(bf16) for n in neg_ls]
        inv_bs = [inv.astype(bf16) for inv in invs]
        mids = [_dot(off, inv_b).astype(bf16) for off, inv_b in zip(offs, inv_bs)]
        invs = [inv + _dot(inv_b, mid) for inv, inv_b, mid in zip(invs, inv_bs, mids)]
        inside = wider
        size *= 2
    return invs


def _gdn_prompt_body(q_ref, k_ref, v_ref, z_ref, gb_ref, gbt_ref, cwq_ref, cwk_ref, cwv_ref, nw_ref,
                     o_ref, st_ref, sb_scr, ob_scr, sk_scr, oq_scr, st_scr, el_scr, s_scr):
    hq = pl.program_id(1)
    nchunk = q_ref.shape[0] // CHUNK
    row = lax.broadcasted_iota(jnp.int32, (CHUNK, CHUNK), 0)
    col = lax.broadcasted_iota(jnp.int32, (CHUNK, CHUNK), 1)
    tri = row >= col
    strict = row > col
    eye = (row == col).astype(f32)
    lane = lax.broadcasted_iota(jnp.int32, (1, LANES), 1)
    rows8 = lax.broadcasted_iota(jnp.int32, (SUBLANES, 1), 0)
    cwq = cwq_ref[...]
    cwk = cwk_ref[...]
    cwv = cwv_ref[...]
    nw = nw_ref[...]

    tile0 = pl.multiple_of((2 * hq) // SUBLANES * SUBLANES, SUBLANES)

    def conv_silu(ref, c, r0, cw):
        x = ref[pl.ds(r0, CHUNK), :]
        p0 = pl.multiple_of(jnp.maximum(r0 - SUBLANES, 0), SUBLANES)
        prev8 = jnp.where(c > 0, ref[pl.ds(p0, SUBLANES), :], 0.0)
        acc = x * cw[GDN_CONV_W - 1:GDN_CONV_W]
        for s in range(1, GDN_CONV_W):
            rolled = pltpu.roll(x, s, 0)
            top = jnp.where(rows8 < s, pltpu.roll(prev8, s, 0), rolled[:SUBLANES])
            shifted = jnp.concatenate([top, rolled[SUBLANES:]], axis=0)
            acc = acc + shifted * cw[GDN_CONV_W - 1 - s:GDN_CONV_W - s]
        return _silu(acc)

    def prepare(i, carry):
        chains = []
        for cc in range(PREP_CHUNKS):
            c = i * PREP_CHUNKS + cc
            r0 = pl.multiple_of(c * CHUNK, CHUNK)
            qn = _l2norm(conv_silu(q_ref, c, r0, cwq)) * GDN_DK ** -0.5
            kn = _l2norm(conv_silu(k_ref, c, r0, cwk))
            vc = conv_silu(v_ref, c, r0, cwv)
            kb = kn.astype(bf16)
            kk = _dot_nt(kb, kb)
            qk = _dot_nt(qn.astype(bf16), kb)
            k_t = kn.T
            gbc = gb_ref[pl.ds(r0, CHUNK), :]
            gbt8 = gbt_ref[pl.ds(tile0, SUBLANES), pl.ds(r0, CHUNK)]
            for hh in range(2):
                hv = 2 * hq + hh
                gc_col = _lane_column(gbc, lane, hv)
                beta_col = _lane_column(gbc, lane, GDN_V_HEADS + hv)
                gc_row = jnp.sum(jnp.where(rows8 == hv % SUBLANES, gbt8, 0.0), axis=0, keepdims=True)
                decay = jnp.where(tri, jnp.exp(jnp.where(tri, gc_col - gc_row, 0.0)), 0.0)
                g_last = gc_row[:, CHUNK - 1:CHUNK]
                el_scr[hh, c] = jnp.broadcast_to(jnp.exp(g_last), (SUBLANES, GDN_DV))
                vh = vc[:, hh * GDN_DV:(hh + 1) * GDN_DV]
                chains.append(dict(
                    hh=hh, c=c,
                    neg_l=jnp.where(strict, -(beta_col * kk * decay), 0.0),
                    vb=(vh * beta_col).astype(bf16),
                    kbg=(kn * (beta_col * jnp.exp(gc_col))).astype(bf16),
                    a=jnp.where(tri, qk * decay, 0.0).astype(bf16),
                    qg=qn * jnp.exp(gc_col),
                    kd=(k_t * jnp.exp(g_last - gc_row)).astype(bf16)))
        invs = _unit_lower_inverses([ch["neg_l"] for ch in chains], eye, row, col)
        inv_bs = [inv.astype(bf16) for inv in invs]
        us = [_dot(inv_b, ch["vb"]).astype(bf16) for ch, inv_b in zip(chains, inv_bs)]
        ws = [_dot(inv_b, ch["kbg"]).astype(bf16) for ch, inv_b in zip(chains, inv_bs)]
        for ch, u, w in zip(chains, us, ws):
            hh, c = ch["hh"], ch["c"]
            sk_scr[hh, c] = _dot(ch["kd"], w).astype(bf16)
            sb_scr[hh, c] = _dot(ch["kd"], u)
            oq_scr[hh, c] = (ch["qg"] - _dot(ch["a"], w)).astype(bf16)
            ob_scr[hh, c] = _dot(ch["a"], u)
        return carry

    def advance(c, carry):
        for hh in range(2):
            state = s_scr[hh]
            state_b = state.astype(bf16)
            st_scr[hh, c] = state_b
            s_scr[hh] = state * el_scr[hh, c][0:1] + (sb_scr[hh, c] - _dot(sk_scr[hh, c], state_b))
        return carry

    def emit(i, carry):
        os_ = []
        for cc in range(OUT_CHUNKS):
            c = i * OUT_CHUNKS + cc
            for hh in range(2):
                os_.append((c, hh, _dot(oq_scr[hh, c], st_scr[hh, c]) + ob_scr[hh, c]))
        for cc in range(OUT_CHUNKS):
            c = i * OUT_CHUNKS + cc
            r0 = pl.multiple_of(c * CHUNK, CHUNK)
            zc = z_ref[pl.ds(r0, CHUNK), :]
            outs = [_gated_rmsnorm(o, nw, zc[:, hh * GDN_DV:(hh + 1) * GDN_DV])
                    for (c2, hh, o) in os_[2 * cc:2 * cc + 2]]
            o_ref[pl.ds(r0, CHUNK), :] = jnp.concatenate(outs, axis=1).astype(o_ref.dtype)
        return carry

    lax.fori_loop(0, nchunk // PREP_CHUNKS, prepare, 0)
    s_scr[...] = jnp.zeros_like(s_scr)
    lax.fori_loop(0, nchunk, advance, 0)
    st_ref[0] = s_scr[...]
    lax.fori_loop(0, nchunk // OUT_CHUNKS, emit, 0)


def _gdn_prompt(p, gb, gbt, conv_w, norm_w, batch, seq, name):
    nqk = GDN_QK_HEADS
    zoff = GDN_CONV_DIM // (2 * GDN_DV)
    nchunk = seq // CHUNK
    tiles_bf = pltpu.VMEM((2, nchunk, CHUNK, CHUNK), bf16)
    tiles_f32 = pltpu.VMEM((2, nchunk, CHUNK, CHUNK), f32)
    return pl.pallas_call(
        _gdn_prompt_body,
        out_shape=(jax.ShapeDtypeStruct((batch * seq, GDN_V_DIM), bf16),
                   jax.ShapeDtypeStruct((batch, GDN_V_HEADS, GDN_DK, GDN_DV), f32)),
        grid=(batch, nqk),
        in_specs=[pl.BlockSpec((seq, GDN_DK), lambda b, h: (b, h)),
                  pl.BlockSpec((seq, GDN_DK), lambda b, h: (b, nqk + h)),
                  pl.BlockSpec((seq, 2 * GDN_DV), lambda b, h: (b, nqk + h)),
                  pl.BlockSpec((seq, 2 * GDN_DV), lambda b, h: (b, zoff + h)),
                  pl.BlockSpec((seq, LANES), lambda b, h: (b, 0)),
                  pl.BlockSpec((LANES, seq), lambda b, h: (0, b)),
                  pl.BlockSpec((GDN_CONV_W, GDN_DK), lambda b, h: (0, h)),
                  pl.BlockSpec((GDN_CONV_W, GDN_DK), lambda b, h: (0, nqk + h)),
                  pl.BlockSpec((GDN_CONV_W, 2 * GDN_DV), lambda b, h: (0, nqk + h)),
                  pl.BlockSpec((1, GDN_DV), lambda b, h: (0, 0))],
        out_specs=(pl.BlockSpec((seq, 2 * GDN_DV), lambda b, h: (b, h)),
                   pl.BlockSpec((1, 2, GDN_DK, GDN_DV), lambda b, h: (b, h, 0, 0))),
        scratch_shapes=[tiles_f32, tiles_f32, tiles_bf, tiles_bf, tiles_bf,
                        pltpu.VMEM((2, nchunk, SUBLANES, GDN_DV), f32),
                        pltpu.VMEM((2, GDN_DK, GDN_DV), f32)],
        compiler_params=_cparams("parallel", "parallel"),
        name=name,
    )(p, p, p, p, gb, gbt, conv_w, conv_w, conv_w, norm_w)


def _gdn_sample_body(q_ref, k_ref, v_ref, z_ref, aq_ref, ak_ref, av_ref, gb_ref, cwq_ref, cwk_ref, cwv_ref,
                     nw_ref, st_ref, *rest):
    o_ref, sto_ref, q_scr, k_scr, v_scr, o_scr = rest[-6:]
    hq = pl.program_id(1)
    rows = q_ref.shape[0]
    nreq = rows // SAMPLE_ROWS
    tok = lax.broadcasted_iota(jnp.int32, (rows, 1), 0) % SAMPLE_ROWS
    lane = lax.broadcasted_iota(jnp.int32, (1, LANES), 1)
    rows8 = lax.broadcasted_iota(jnp.int32, (SAMPLE_ROWS, 1), 0)
    nw = nw_ref[...]
    hist = GDN_CONV_W - 1

    def conv_silu(a, x, cw):
        acc = jnp.zeros_like(x)
        for w in range(GDN_CONV_W):
            ra = a if w == 0 else pltpu.roll(a, rows - w, 0)
            rx = x if w == hist else pltpu.roll(x, hist - w, 0)
            acc = acc + jnp.where(tok + w < hist, ra, rx) * cw[w:w + 1]
        return _silu(acc)

    q_scr[...] = _l2norm(conv_silu(aq_ref[...], q_ref[...], cwq_ref[...])) * GDN_DK ** -0.5
    k_scr[...] = _l2norm(conv_silu(ak_ref[...], k_ref[...], cwk_ref[...]))
    v_scr[...] = conv_silu(av_ref[...], v_ref[...], cwv_ref[...])
    pad = jnp.zeros((LANES - 2 * SAMPLE_ROWS, GDN_DK), f32)

    def requests(i, carry):
        work = []
        for rr in range(REQ_LOCK):
            bi = i * REQ_LOCK + rr
            r0 = pl.multiple_of(bi * SAMPLE_ROWS, SAMPLE_ROWS)
            k8 = k_scr[pl.ds(r0, SAMPLE_ROWS), :]
            q8 = q_scr[pl.ds(r0, SAMPLE_ROWS), :]
            v8 = v_scr[pl.ds(r0, SAMPLE_ROWS), :]
            gb8 = gb_ref[pl.ds(r0, SAMPLE_ROWS), :]
            kq_t = jnp.concatenate([k8, q8, pad], axis=0).T
            for hh in range(2):
                hv = 2 * hq + hh
                work.append(dict(bi=bi, hh=hh, r0=r0, kq_t=kq_t, v=v8[:, hh * GDN_DV:(hh + 1) * GDN_DV],
                                 g=_lane_column(gb8, lane, 2 * GDN_V_HEADS + hv),
                                 beta=_lane_column(gb8, lane, GDN_V_HEADS + hv),
                                 state=st_ref[bi, hh], o=jnp.zeros((SAMPLE_ROWS, GDN_DV), f32)))
        for t in range(GDN_CONV_W):
            for w in work:
                w["state"] = w["state"] * jnp.exp(w["g"][t:t + 1])
            for w in work:
                k_col = w["kq_t"][:, t:t + 1]
                ks = jnp.sum(w["state"] * k_col, axis=0, keepdims=True)
                delta = (w["v"][t:t + 1] - ks) * w["beta"][t:t + 1]
                w["state"] = w["state"] + k_col * delta
            for w in work:
                q_col = w["kq_t"][:, SAMPLE_ROWS + t:SAMPLE_ROWS + t + 1]
                o_t = jnp.sum(w["state"] * q_col, axis=0, keepdims=True)
                w["o"] = jnp.where(rows8 == t, o_t, w["o"])
        for w in work:
            sto_ref[w["bi"], w["hh"]] = w["state"]
        for w0, w1 in zip(work[0::2], work[1::2]):
            z8 = z_ref[pl.ds(w0["r0"], SAMPLE_ROWS), :]
            o_scr[pl.ds(w0["r0"], SAMPLE_ROWS), :] = jnp.concatenate(
                [_gated_rmsnorm(w["o"], nw, z8[:, w["hh"] * GDN_DV:(w["hh"] + 1) * GDN_DV]) for w in (w0, w1)], axis=1)
        return carry

    lax.fori_loop(0, nreq // REQ_LOCK, requests, 0)
    o_ref[...] = o_scr[...].astype(o_ref.dtype)


def _gdn_sample(p, conv_buf, gb, conv_w, norm_w, state, new_state, layer, nreq_blk, name):
    assert GDN_CONV_W == 4 and SAMPLE_ROWS == 8
    nqk = GDN_QK_HEADS
    zoff = GDN_CONV_DIM // (2 * GDN_DV)
    nreq = state.shape[1]
    rows = nreq_blk * SAMPLE_ROWS
    st_spec = pl.BlockSpec((None, nreq_blk, 2, GDN_DK, GDN_DV), lambda i, h: (layer, i, h, 0, 0))
    carried = [] if new_state is None else [new_state]
    return pl.pallas_call(
        _gdn_sample_body,
        out_shape=(jax.ShapeDtypeStruct((nreq * SAMPLE_ROWS, GDN_V_DIM), bf16),
                   jax.ShapeDtypeStruct(state.shape, f32)),
        grid=(nreq // nreq_blk, nqk),
        in_specs=[pl.BlockSpec((rows, GDN_DK), lambda i, h: (i, h)),
                  pl.BlockSpec((rows, GDN_DK), lambda i, h: (i, nqk + h)),
                  pl.BlockSpec((rows, 2 * GDN_DV), lambda i, h: (i, nqk + h)),
                  pl.BlockSpec((rows, 2 * GDN_DV), lambda i, h: (i, zoff + h)),
                  pl.BlockSpec((rows, GDN_DK), lambda i, h: (i, h)),
                  pl.BlockSpec((rows, GDN_DK), lambda i, h: (i, nqk + h)),
                  pl.BlockSpec((rows, 2 * GDN_DV), lambda i, h: (i, nqk + h)),
                  pl.BlockSpec((rows, LANES), lambda i, h: (i, 0)),
                  pl.BlockSpec((GDN_CONV_W, GDN_DK), lambda i, h: (0, h)),
                  pl.BlockSpec((GDN_CONV_W, GDN_DK), lambda i, h: (0, nqk + h)),
                  pl.BlockSpec((GDN_CONV_W, 2 * GDN_DV), lambda i, h: (0, nqk + h)),
                  pl.BlockSpec((1, GDN_DV), lambda i, h: (0, 0)),
                  st_spec] + [pl.BlockSpec(memory_space=pl.ANY)] * len(carried),
        out_specs=(pl.BlockSpec((rows, 2 * GDN_DV), lambda i, h: (i, h)), st_spec),
        scratch_shapes=[pltpu.VMEM((rows, GDN_DK), f32), pltpu.VMEM((rows, GDN_DK), f32),
                        pltpu.VMEM((rows, 2 * GDN_DV), f32), pltpu.VMEM((rows, 2 * GDN_DV), f32)],
        input_output_aliases={13: 1} if carried else {},
        compiler_params=_cparams("parallel", "parallel"),
        name=name,
    )(p, p, p, p, conv_buf, conv_buf, conv_buf, gb, conv_w, conv_w, conv_w, norm_w, state, *carried)


def _swa_prompt_body(*refs):
    qkv_refs = refs[:3 * SWA_GROUPS]
    bias_ref, o_ref, o_scr, l_scr = refs[3 * SWA_GROUPS:]
    seq = o_ref.shape[0]
    lane = lax.broadcasted_iota(jnp.int32, (1, LANES), 1)
    head0 = lane < SWA_HEAD_DIM
    blk = SWA_KEYS
    for g, d in enumerate(SWA_DILATIONS):
        q_ref, k_ref, v_ref = qkv_refs[3 * g:3 * g + 3]
        nb = seq // (d * blk)

        def rows_of(start, d=d):
            return pl.ds(start, blk) if d == 1 else pl.ds(start, blk, stride=d)

        def blocks(i, carry, g=g, d=d, nb=nb, q_ref=q_ref, k_ref=k_ref, v_ref=v_ref, rows_of=rows_of):
            has_prev = nb > 1
            work = []
            for bb in range(ATT_BLOCKS):
                idx = i * ATT_BLOCKS + bb
                r = idx // nb
                b = idx % nb
                cur = rows_of(r + d * blk * b)
                q = q_ref[cur, :] * SWA_HEAD_DIM ** -0.5
                kc = k_ref[cur, :].astype(bf16)
                vc = v_ref[cur, :].astype(bf16)
                kp = vp = no_prev = None
                if has_prev:
                    prev = rows_of(r + d * blk * jnp.maximum(b - 1, 0))
                    no_prev = jnp.where(b == 0, -jnp.inf, 0.0).astype(f32)
                    kp = k_ref[prev, :].astype(bf16)
                    vp = v_ref[prev, :].astype(bf16)
                for hh in range(2):
                    qh = jnp.where(head0 if hh == 0 else ~head0, q, 0.0).astype(bf16)
                    work.append(dict(cur=cur, hh=hh, qh=qh, kc=kc, vc=vc, kp=kp, vp=vp, no_prev=no_prev))
            for w in work:
                w["s_c"] = _dot_nt(w["qh"], w["kc"]) + bias_ref[g, w["hh"]][:, blk:]
            if has_prev:
                for w in work:
                    w["s_p"] = _dot_nt(w["qh"], w["kp"]) + (bias_ref[g, w["hh"]][:, :blk] + w["no_prev"])
            for w in work:
                m = jnp.max(jnp.maximum(w["s_c"], w["s_p"]) if has_prev else w["s_c"], axis=-1, keepdims=True)
                p_c = jnp.exp(w["s_c"] - m)
                w["p_c"] = p_c.astype(bf16)
                if has_prev:
                    p_p = jnp.exp(w["s_p"] - m)
                    w["p_p"] = p_p.astype(bf16)
                    p_c = p_c + p_p
                w["m"], w["den"] = m, jnp.sum(p_c, axis=-1, keepdims=True)
            for w in work:
                w["pv"] = _dot(w["p_c"], w["vc"])
            if has_prev:
                for w in work:
                    w["pv"] = w["pv"] + _dot(w["p_p"], w["vp"])
            for w0, w1 in zip(work[0::2], work[1::2]):
                o_scr[g, w0["cur"], :] = jnp.where(head0, w0["pv"] / w0["den"], w1["pv"] / w1["den"])
                l_scr[g, w0["cur"], :] = jnp.where(head0, w0["m"] + jnp.log(w0["den"]), w1["m"] + jnp.log(w1["den"]))
            return carry

        lax.fori_loop(0, seq // (blk * ATT_BLOCKS), blocks, 0)

    def merge(c, carry):
        rr = pl.ds(pl.multiple_of(c * blk, blk), blk)
        ls = [l_scr[g, rr, :] for g in range(SWA_GROUPS)]
        mx = jnp.maximum(jnp.maximum(ls[0], ls[1]), ls[2])
        ws = [jnp.exp(l - mx) for l in ls]
        num = ws[0] * o_scr[0, rr, :] + ws[1] * o_scr[1, rr, :] + ws[2] * o_scr[2, rr, :]
        o_ref[rr, :] = (num / (ws[0] + ws[1] + ws[2])).astype(o_ref.dtype)
        return carry

    lax.fori_loop(0, seq // blk, merge, 0)


def _swa_prompt(p, bias_t, batch, seq, name):
    npair = SWA_INNER // LANES
    in_specs = []
    for g in range(SWA_GROUPS):
        for c in range(3):
            off = (g * 3 + c) * npair
            in_specs.append(pl.BlockSpec((seq, LANES), lambda b, h, off=off: (b, off + h)))
    in_specs.append(pl.BlockSpec((SWA_GROUPS, 2, SWA_KEYS, 2 * SWA_KEYS), lambda b, h: (0, h, 0, 0)))
    return pl.pallas_call(
        _swa_prompt_body,
        out_shape=jax.ShapeDtypeStruct((batch * seq, SWA_INNER), bf16),
        grid=(batch, npair),
        in_specs=in_specs,
        out_specs=pl.BlockSpec((seq, LANES), lambda b, h: (b, h)),
        scratch_shapes=[pltpu.VMEM((SWA_GROUPS, seq, LANES), f32), pltpu.VMEM((SWA_GROUPS, seq, LANES), f32)],
        compiler_params=_cparams("parallel", "parallel"),
        name=name,
    )(*([p] * (3 * SWA_GROUPS)), bias_t)


def _swa_sample_body(*refs):
    qkv_refs = refs[:3 * SWA_GROUPS]
    cache_refs = refs[3 * SWA_GROUPS:4 * SWA_GROUPS]
    tb_refs = refs[4 * SWA_GROUPS:5 * SWA_GROUPS]
    tn_ref, o_ref = refs[5 * SWA_GROUPS:]
    nhead = cache_refs[0].shape[1]
    heads = [slice(h * SAMPLE_ROWS, (h + 1) * SAMPLE_ROWS) for h in range(nhead)]

    def by_head(ref):
        x = ref[...]
        return jnp.concatenate([x[:, h * SWA_HEAD_DIM:(h + 1) * SWA_HEAD_DIM] for h in range(nhead)], axis=0)

    qs, s_cache, s_new = [], [], []
    for g in range(SWA_GROUPS):
        q = (by_head(qkv_refs[3 * g]) * SWA_HEAD_DIM ** -0.5).astype(bf16)
        qs.append(q)
        s_cache.append(jnp.concatenate(
            [_dot(q[heads[h]], cache_refs[g][0, h].astype(bf16)) for h in range(nhead)], axis=0) + tb_refs[g][...])
    for g in range(SWA_GROUPS):
        k_new = by_head(qkv_refs[3 * g + 1]).astype(bf16)
        s_new.append(_dot_nt(qs[g], k_new) + tn_ref[g])
    ms, dens, ps, pns = [], [], [], []
    for g in range(SWA_GROUPS):
        m = jnp.maximum(jnp.max(s_cache[g], axis=-1, keepdims=True), jnp.max(s_new[g], axis=-1, keepdims=True))
        p = jnp.exp(s_cache[g] - m)
        pn = jnp.exp(s_new[g] - m)
        ms.append(m)
        dens.append(jnp.sum(p, axis=-1, keepdims=True) + jnp.sum(pn, axis=-1, keepdims=True))
        ps.append(p.astype(bf16))
        pns.append(pn.astype(bf16))
    pvs = []
    for g in range(SWA_GROUPS):
        pvs.append(jnp.concatenate(
            [_dot_nt(ps[g][heads[h]], cache_refs[g][1, h].astype(bf16)) for h in range(nhead)], axis=0))
    for g in range(SWA_GROUPS):
        v_new = by_head(qkv_refs[3 * g + 2]).astype(bf16)
        pvs[g] = pvs[g] + _dot(pns[g], v_new)
    o_g = [pv / den for pv, den in zip(pvs, dens)]
    l_g = [m + jnp.log(den) for m, den in zip(ms, dens)]
    mx = jnp.maximum(jnp.maximum(l_g[0], l_g[1]), l_g[2])
    ws = [jnp.exp(l - mx) for l in l_g]
    merged = (ws[0] * o_g[0] + ws[1] * o_g[1] + ws[2] * o_g[2]) / (ws[0] + ws[1] + ws[2])
    o_ref[...] = jnp.concatenate([merged[heads[h]] for h in range(nhead)], axis=1)


def _swa_sample(p, caches_t, layer, tb, tn, name):
    nreq = p.shape[0] // SAMPLE_ROWS
    nhalf = 2
    nhead = SWA_HEADS // nhalf
    nrow = nhead * SAMPLE_ROWS
    width = nhead * SWA_HEAD_DIM
    in_specs = [pl.BlockSpec((SAMPLE_ROWS, width), lambda h, b, blk=blk: (b, nhalf * blk + h))
                for blk in range(3 * SWA_GROUPS)]
    for c in caches_t:
        in_specs.append(pl.BlockSpec((None, None, 2, nhead, SWA_HEAD_DIM, c.shape[-1]),
                                     lambda h, b: (layer, b, 0, h, 0, 0)))
    for t in tb:
        in_specs.append(pl.BlockSpec((nrow, t.shape[-1]), lambda h, b: (h, 0)))
    in_specs.append(pl.BlockSpec((None, SWA_GROUPS, nrow, nrow), lambda h, b: (h, 0, 0, 0)))
    return pl.pallas_call(
        _swa_sample_body,
        out_shape=jax.ShapeDtypeStruct((nreq * SAMPLE_ROWS, SWA_INNER), f32),
        grid=(nhalf, nreq),
        in_specs=in_specs,
        out_specs=pl.BlockSpec((SAMPLE_ROWS, width), lambda h, b: (b, h)),
        compiler_params=_cparams("parallel", "parallel"),
        name=name,
    )(*([p] * (3 * SWA_GROUPS)), *caches_t, *tb, tn)


def _window_t_body(p_ref, *rest):
    o_ref = rest[-1]
    o_ref[...] = p_ref[...].T


def _window_t(p, stacked, layer, nlayer, g, batch, seq, keep, name):
    tk = min(keep, 512)
    first = (seq - keep) // tk
    per_req = seq // tk
    col0 = 3 * g + 1
    carried = [] if stacked is None else [stacked]
    return pl.pallas_call(
        _window_t_body,
        out_shape=jax.ShapeDtypeStruct((nlayer, batch, 2, SWA_INNER, keep), f32),
        grid=(batch, 2, keep // tk),
        in_specs=[pl.BlockSpec((tk, SWA_INNER), lambda b, c, i: (b * per_req + first + i, col0 + c))]
        + [pl.BlockSpec(memory_space=pl.ANY)] * len(carried),
        out_specs=pl.BlockSpec((None, None, None, SWA_INNER, tk), lambda b, c, i: (layer, b, c, 0, i)),
        input_output_aliases={1: 0} if carried else {},
        compiler_params=_cparams("parallel", "parallel", "parallel"),
        name=name,
    )(p, *carried)


def _t5_bucket(dist):
    max_exact = N_BUCKETS // 2
    dd = jnp.maximum(dist, 1).astype(f32)
    large = max_exact + (jnp.log(dd / max_exact) / math.log(MAX_DISTANCE / max_exact)
                         * (N_BUCKETS - max_exact)).astype(jnp.int32)
    return jnp.where(dist < max_exact, dist, jnp.minimum(large, N_BUCKETS - 1))


def _select_rows(table, idx):
    onehot = (idx[..., None] == jnp.arange(table.shape[0])).astype(f32)
    return jnp.einsum("...n,nc->...c", onehot, table, precision=lax.Precision.HIGHEST)


def _bias_tables(rel_bias, ntok, cache_lens):
    neg = -jnp.inf
    i = jnp.arange(SWA_KEYS)[:, None]
    jj = jnp.arange(2 * SWA_KEYS)[None, :]
    steps = SWA_KEYS + i - jj
    band = (steps >= 0) & (steps <= SWA_KEYS)
    t_row = jnp.where(jnp.arange(SAMPLE_ROWS) < ntok, jnp.arange(SAMPLE_ROWS), 0)[:, None]
    t_new = jnp.arange(SAMPLE_ROWS)[None, :]
    same_head = jnp.eye(SWA_HEADS, dtype=bool)[:, None, :, None]
    nhalf = SWA_HEADS // 2 * SAMPLE_ROWS
    prompt_t, cached_t, new_t = [], [], []
    for g, d in enumerate(SWA_DILATIONS):
        dist = jnp.arange(SWA_KEYS + 1, dtype=jnp.int32) * d
        bias_g = _select_rows(rel_bias[:, g * SWA_HEADS:(g + 1) * SWA_HEADS], _t5_bucket(dist))
        toeplitz = _select_rows(bias_g, jnp.clip(steps, 0, SWA_KEYS))
        prompt_t.append(jnp.where(band[None], jnp.transpose(toeplitz, (2, 0, 1)), neg))
        length = cache_lens[g]
        back = length + t_row - jnp.arange(length)[None, :]
        ok = (back % d == 0) & (back >= d) & (back <= SWA_KEYS * d)
        vals = _select_rows(bias_g, jnp.clip(back // d, 0, SWA_KEYS))
        cached = jnp.where(ok[None], jnp.transpose(vals, (2, 0, 1)), neg)
        cached_t.append(cached.reshape(SWA_HEADS * SAMPLE_ROWS, length))
        back = t_row - t_new
        ok = (back % d == 0) & (back >= 0) & (t_new < ntok)
        vals = _select_rows(bias_g, jnp.clip(back // d, 0, SWA_KEYS))
        fresh = jnp.where(ok[None], jnp.transpose(vals, (2, 0, 1)), neg)
        full = jnp.where(same_head, fresh[:, :, None, :], neg).reshape(2 * nhalf, 2 * nhalf)
        new_t.append(jnp.stack([full[:nhalf, :nhalf], full[nhalf:, nhalf:]]))
    return jnp.stack(prompt_t), cached_t, jnp.stack(new_t, axis=1)


def kernel(x_prompt, x_sample, state_gdn, state_conv, cache_win0, cache_win1, cache_win2, w_in_gdn, conv_w_gdn,
           a_log_gdn, dt_bias_gdn, norm_gdn, w_out_gdn, w_in_swa, w_out_swa, rel_bias, w_ffn_in, w_ffn_out,
           ln_mix_g, ln_mix_b, ln_ffn_g, ln_ffn_b):
    batch, seq, _ = x_prompt.shape
    nreq, ntok, _ = x_sample.shape
    caches = (cache_win0, cache_win1, cache_win2)
    mp = batch * seq
    tm_p = 1024
    tm_s = nreq * SAMPLE_ROWS

    xp = x_prompt.reshape(mp, D_MODEL)
    xs = jnp.pad(x_sample, ((0, 0), (0, SAMPLE_ROWS - ntok), (0, 0))).reshape(tm_s, D_MODEL)
    caches_t = tuple(jnp.transpose(c, (0, 1, 3, 4, 5, 2)) for c in caches)
    bias_p, bias_s, bias_new = _bias_tables(rel_bias, ntok, [c.shape[2] for c in caches])

    gdn_p, conv_p, conv_s = [], [], []
    gdn_s = None
    win_p = [None] * SWA_GROUPS
    win_s = [[] for _ in range(SWA_GROUPS)]
    for i in range(DEPTH):
        j = i // N_MIXERS
        ln_g, ln_b = ln_mix_g[i][None], ln_mix_b[i][None]
        if i % N_MIXERS == 0:
            w = w_in_gdn[j]
            w_qkvz = w[:, :GDN_QKVZ_DIM].astype(bf16)
            w_a = w[:, GDN_QKVZ_DIM:GDN_QKVZ_DIM + GDN_V_HEADS]
            w_b = w[:, GDN_QKVZ_DIM + GDN_V_HEADS:]
            fill = jnp.zeros((D_MODEL, LANES - 3 * GDN_V_HEADS), f32)
            w_ab = jnp.concatenate([w_a, w_b, w_a, fill], axis=1).astype(bf16)
            zpad = jnp.zeros((LANES - 3 * GDN_V_HEADS,), f32)
            zh = jnp.zeros((GDN_V_HEADS,), f32)
            alog = jnp.concatenate([a_log_gdn[j], zh, a_log_gdn[j], zpad])[None]
            dtb = jnp.concatenate([dt_bias_gdn[j], zh, dt_bias_gdn[j], zpad])[None]
            conv_w = conv_w_gdn[j]
            norm_w = norm_gdn[j][None]
            w_out = w_out_gdn[j].astype(bf16)

            pp = _proj(xp, w_qkvz, tm_p, 1024, f"gdn_in_p{j}")
            ps = _proj(xs, w_qkvz, tm_s, 1024, f"gdn_in_s{j}")
            gb_p, gbt_p = _gates(xp, w_ab, alog, dtb, tm_p, f"gdn_gates_p{j}")
            gb_s, _ = _gates(xs, w_ab, alog, dtb, tm_s, f"gdn_gates_s{j}")
            o_p, st_p = _gdn_prompt(pp, gb_p, gbt_p, conv_w, norm_w, batch, seq, f"gdn_prompt{j}")
            cbuf = jnp.pad(state_conv[j], ((0, 0), (0, SAMPLE_ROWS - (GDN_CONV_W - 1)), (0, 0)))
            o_s, gdn_s = _gdn_sample(ps, cbuf.reshape(tm_s, GDN_CONV_DIM), gb_s, conv_w, norm_w, state_gdn, gdn_s,
                                     j, 8, f"gdn_sample{j}")
            xp = _mm_ln(o_p, w_out, xp, ln_g, ln_b, tm_p, f"gdn_out_p{j}")
            xs = _mm_ln(o_s, w_out, xs, ln_g, ln_b, tm_s, f"gdn_out_s{j}")
            gdn_p.append(st_p)
            conv_p.append(pp.reshape(batch, seq, GDN_QKVZ_DIM)[:, seq - (GDN_CONV_W - 1):, :GDN_CONV_DIM])
            conv_s.append(ps.reshape(nreq, SAMPLE_ROWS, GDN_QKVZ_DIM)[:, ntok - (GDN_CONV_W - 1):ntok, :GDN_CONV_DIM])
        else:
            w_in = w_in_swa[j].astype(bf16)
            w_out = w_out_swa[j].astype(bf16)
            pp = _proj(xp, w_in, tm_p, 1024, f"swa_in_p{j}")
            ps = _proj(xs, w_in, tm_s, 1024, f"swa_in_s{j}")
            a_p = _swa_prompt(pp, bias_p, batch, seq, f"swa_prompt{j}")
            a_s = _swa_sample(ps, caches_t, j, bias_s, bias_new, f"swa_sample{j}")
            ps3 = ps.reshape(nreq, SAMPLE_ROWS, SWA_IN_DIM)
            xp = _mm_ln(a_p, w_out, xp, ln_g, ln_b, tm_p, f"swa_out_p{j}")
            xs = _mm_ln(a_s, w_out, xs, ln_g, ln_b, tm_s, f"swa_out_s{j}")
            for g in range(SWA_GROUPS):
                keep = min(SWA_KEYS * SWA_DILATIONS[g], seq)
                win_p[g] = _window_t(pp, win_p[g], j, DEPTH // N_MIXERS, g, batch, seq, keep, f"win_p{g}_{j}")
                lo, hi = (3 * g + 1) * SWA_INNER, (3 * g + 3) * SWA_INNER
                win_s[g].append(ps3[:, :ntok, lo:hi].reshape(nreq, ntok, 2, SWA_HEADS, SWA_HEAD_DIM))
        w_fi = w_ffn_in[i].astype(bf16)
        w_fo = w_ffn_out[i].astype(bf16)
        xp = _ffn(xp, w_fi, w_fo, ln_ffn_g[i][None], ln_ffn_b[i][None], tm_p, f"ffn_p{i}")
        xs = _ffn(xs, w_fi, w_fo, ln_ffn_g[i][None], ln_ffn_b[i][None], tm_s, f"ffn_s{i}")

    y_p = xp.reshape(batch, seq, D_MODEL)
    y_s = xs.reshape(nreq, SAMPLE_ROWS, D_MODEL)[:, :ntok]
    return (y_p, y_s, jnp.stack(gdn_p), jnp.stack(conv_p),
            *[jnp.transpose(w.reshape(w.shape[:3] + (SWA_HEADS, SWA_HEAD_DIM, w.shape[-1])), (0, 1, 5, 2, 3, 4))
              for w in win_p],
            gdn_s, jnp.stack(conv_s),
            jnp.stack(win_s[0]), jnp.stack(win_s[1]), jnp.stack(win_s[2]))
```

```python
import functools
import math

import jax
import jax.numpy as jnp
from jax import lax
from jax.experimental import pallas as pl
from jax.experimental.pallas import tpu as pltpu

f32 = jnp.float32
bf16 = jnp.bfloat16

D_MODEL = 1024
DEPTH = 4
N_MIXERS = 2
GDN_QK_HEADS = 8
GDN_V_HEADS = 16
GDN_DK = 128
GDN_DV = 128
GDN_CONV_W = 4
GDN_QK_DIM = GDN_QK_HEADS * GDN_DK
GDN_V_DIM = GDN_V_HEADS * GDN_DV
GDN_CONV_DIM = 2 * GDN_QK_DIM + GDN_V_DIM
GDN_QKVZ_DIM = GDN_CONV_DIM + GDN_V_DIM
SWA_DILATIONS = (1, 4, 16)
SWA_GROUPS = 3
SWA_HEADS = 16
SWA_HEAD_DIM = 64
SWA_KEYS = 128
SWA_INNER = SWA_HEADS * SWA_HEAD_DIM
SWA_IN_DIM = SWA_GROUPS * 3 * SWA_INNER
N_BUCKETS = 32
MAX_DISTANCE = 2048
D_FF = 2816
DEEPNORM_ALPHA = (2 * DEPTH) ** 0.25
LN_EPS = 1e-5
RMS_EPS = 1e-6
L2_EPS = 1e-6

LANES = 128
SUBLANES = 8
VMEM_LIMIT = 56 * 1024 * 1024

CHUNK = 128
PREP_CHUNKS = 8
OUT_CHUNKS = 4
ATT_BLOCKS = 8
REQ_LOCK = 4
SAMPLE_ROWS = 8
FFN_TILE = 256
FFN_ROWS = 256
PROJ_TILE = 1024
PROJ_ROWS = 256


def _cparams(*sem):
    return pltpu.CompilerParams(dimension_semantics=sem, vmem_limit_bytes=VMEM_LIMIT)


def _dot(a, b):
    return jnp.dot(a, b, preferred_element_type=f32)


def _dot_nt(a, b):
    return lax.dot_general(a, b, (((1,), (1,)), ((), ())), preferred_element_type=f32)


def _sigmoid(x):
    return 1.0 / (1.0 + jnp.exp(-x))


def _silu(x):
    return x * _sigmoid(x)


def _layer_norm(y, g, b):
    mu = jnp.mean(y, axis=-1, keepdims=True)
    yc = y - mu
    var = jnp.mean(yc * yc, axis=-1, keepdims=True)
    return yc * lax.rsqrt(var + LN_EPS) * g + b


def _proj_body(x_ref, w_ref, o_ref, xb_ref):
    @pl.when(pl.program_id(1) == 0)
    def _():
        xb_ref[...] = x_ref[...].astype(bf16)

    o_ref[...] = _dot(xb_ref[...], w_ref[...]).astype(o_ref.dtype)


def _proj(x, w, tm, tn, name):
    m, k = x.shape
    n = w.shape[1]
    return pl.pallas_call(
        _proj_body,
        out_shape=jax.ShapeDtypeStruct((m, n), f32),
        grid=(m // tm, n // tn),
        in_specs=[pl.BlockSpec((tm, k), lambda i, j: (i, 0)),
                  pl.BlockSpec((k, tn), lambda i, j: (0, j))],
        out_specs=pl.BlockSpec((tm, tn), lambda i, j: (i, j)),
        scratch_shapes=[pltpu.VMEM((tm, k), bf16)],
        compiler_params=_cparams("parallel", "arbitrary"),
        name=name,
    )(x, w)


def _proj_rows_body(x_ref, w_ref, o_ref):
    xb = x_ref[...].astype(bf16)
    for j in range(w_ref.shape[1] // PROJ_TILE):
        cols = slice(j * PROJ_TILE, (j + 1) * PROJ_TILE)
        o_ref[:, cols] = _dot(xb, w_ref[:, cols])


def _proj_rows(x, w, tm, name):
    m, k = x.shape
    n = w.shape[1]
    return pl.pallas_call(
        _proj_rows_body,
        out_shape=jax.ShapeDtypeStruct((m, n), f32),
        grid=(m // tm,),
        in_specs=[pl.BlockSpec((tm, k), lambda i: (i, 0)),
                  pl.BlockSpec((k, n), lambda i: (0, 0), pipeline_mode=pl.Buffered(1))],
        out_specs=pl.BlockSpec((tm, n), lambda i: (i, 0)),
        compiler_params=_cparams("parallel"),
        name=name,
    )(x, w)


def _mm_ln_body(a_ref, w_ref, x_ref, g_ref, b_ref, o_ref):
    h = _dot(a_ref[...].astype(bf16), w_ref[...])
    o_ref[...] = _layer_norm(DEEPNORM_ALPHA * x_ref[...] + h, g_ref[...], b_ref[...])


def _mm_ln(a, w, x, g, b, tm, name):
    m, k = a.shape
    return pl.pallas_call(
        _mm_ln_body,
        out_shape=jax.ShapeDtypeStruct((m, D_MODEL), f32),
        grid=(m // tm,),
        in_specs=[pl.BlockSpec((tm, k), lambda i: (i, 0)),
                  pl.BlockSpec((k, D_MODEL), lambda i: (0, 0)),
                  pl.BlockSpec((tm, D_MODEL), lambda i: (i, 0)),
                  pl.BlockSpec((1, D_MODEL), lambda i: (0, 0)),
                  pl.BlockSpec((1, D_MODEL), lambda i: (0, 0))],
        out_specs=pl.BlockSpec((tm, D_MODEL), lambda i: (i, 0)),
        compiler_params=_cparams("parallel"),
        name=name,
    )(a, w, x, g, b)


def _ffn_body(x_ref, wi_ref, wo_ref, g_ref, b_ref, o_ref):
    nf = D_FF // FFN_TILE
    nrow = min(FFN_ROWS, x_ref.shape[0])
    assert x_ref.shape[0] % nrow == 0

    def rows(r, carry):
        rr = pl.ds(pl.multiple_of(r * nrow, nrow), nrow)
        x = x_ref[rr, :]
        xb = x.astype(bf16)
        acts = []
        for j in range(nf):
            h1 = _dot(xb, wi_ref[:, j * FFN_TILE:(j + 1) * FFN_TILE])
            h2 = _dot(xb, wi_ref[:, D_FF + j * FFN_TILE:D_FF + (j + 1) * FFN_TILE])
            acts.append((_silu(h1) * h2).astype(bf16))
        h = _dot(jnp.concatenate(acts, axis=1), wo_ref[...])
        o_ref[rr, :] = _layer_norm(DEEPNORM_ALPHA * x + h, g_ref[...], b_ref[...])
        return carry

    lax.fori_loop(0, x_ref.shape[0] // nrow, rows, 0)


def _ffn(x, w_in, w_out, g, b, tm, name):
    m = x.shape[0]
    once = dict(pipeline_mode=pl.Buffered(1))
    return pl.pallas_call(
        _ffn_body,
        out_shape=jax.ShapeDtypeStruct((m, D_MODEL), f32),
        grid=(m // tm,),
        in_specs=[pl.BlockSpec((tm, D_MODEL), lambda i: (i, 0)),
                  pl.BlockSpec((D_MODEL, 2 * D_FF), lambda i: (0, 0), **once),
                  pl.BlockSpec((D_FF, D_MODEL), lambda i: (0, 0), **once),
                  pl.BlockSpec((1, D_MODEL), lambda i: (0, 0)),
                  pl.BlockSpec((1, D_MODEL), lambda i: (0, 0))],
        out_specs=pl.BlockSpec((tm, D_MODEL), lambda i: (i, 0)),
        compiler_params=_cparams("parallel"),
        name=name,
    )(x, w_in, w_out, g, b)


def _gates_body(x_ref, w_ref, al_ref, dt_ref, gb_ref, gbt_ref):
    ab = _dot(x_ref[...].astype(bf16), w_ref[...])
    a = ab + dt_ref[...]
    softplus = jnp.maximum(a, 0.0) + jnp.log1p(jnp.exp(-jnp.abs(a)))
    g = -jnp.exp(al_ref[...]) * softplus
    beta = _sigmoid(ab)
    tm = ab.shape[0]
    row = lax.broadcasted_iota(jnp.int32, (CHUNK, CHUNK), 0)
    col = lax.broadcasted_iota(jnp.int32, (CHUNK, CHUNK), 1)
    tril = (row >= col).astype(f32)
    gc = jnp.concatenate(
        [jnp.dot(tril, g[c * CHUNK:(c + 1) * CHUNK], precision=lax.Precision.HIGHEST,
                 preferred_element_type=f32) for c in range(tm // CHUNK)], axis=0)
    lane = lax.broadcasted_iota(jnp.int32, (1, LANES), 1)
    gb = jnp.where(lane < GDN_V_HEADS, gc, jnp.where(lane < 2 * GDN_V_HEADS, beta, g))
    gb_ref[...] = gb
    gbt_ref[...] = gb.T


def _gates(x, w_ab, alog, dtb, tm, name):
    m = x.shape[0]
    return pl.pallas_call(
        _gates_body,
        out_shape=(jax.ShapeDtypeStruct((m, LANES), f32), jax.ShapeDtypeStruct((LANES, m), f32)),
        grid=(m // tm,),
        in_specs=[pl.BlockSpec((tm, D_MODEL), lambda i: (i, 0)),
                  pl.BlockSpec((D_MODEL, LANES), lambda i: (0, 0)),
                  pl.BlockSpec((1, LANES), lambda i: (0, 0)),
                  pl.BlockSpec((1, LANES), lambda i: (0, 0))],
        out_specs=(pl.BlockSpec((tm, LANES), lambda i: (i, 0)),
                   pl.BlockSpec((LANES, tm), lambda i: (0, i))),
        compiler_params=_cparams("parallel"),
        name=name,
    )(x, w_ab, alog, dtb)


def _lane_column(x, lane, idx):
    return jnp.sum(jnp.where(lane == idx, x, 0.0), axis=1, keepdims=True)


def _l2norm(x):
    return x * lax.rsqrt(jnp.sum(x * x, axis=-1, keepdims=True) + L2_EPS)


def _gated_rmsnorm(o, nw, z):
    return o * lax.rsqrt(jnp.mean(o * o, axis=-1, keepdims=True) + RMS_EPS) * nw * _silu(z)


def _unit_lower_inverses(neg_ls, eye, row, col):
    def same_block(size):
        return (row // size) == (col // size)

    base = 2 * SUBLANES
    inside = same_block(base)
    powers = [jnp.where(inside, n, 0.0) for n in neg_ls]
    invs = [eye + p for p in powers]
    span = 2
    while span < base:
        pbs = [p.astype(bf16) for p in powers]
        powers = [_dot(pb, pb) for pb in pbs]
        invs = [inv + _dot(inv.astype(bf16), p.astype(bf16)) for inv, p in zip(invs, powers)]
        span *= 2
    size = base
    while size < CHUNK:
        wider = same_block(2 * size)
        offs = [jnp.where(wider & ~inside, n, 0.0).astype(bf16) for n in neg_ls]
        inv_bs = [inv.astype(bf16) for inv in invs]
        mids = [_dot(off, inv_b).astype(bf16) for off, inv_b in zip(offs, inv_bs)]
        invs = [inv + _dot(inv_b, mid) for inv, inv_b, mid in zip(invs, inv_bs, mids)]
        inside = wider
        size *= 2
    return invs


def _gdn_prompt_body(q_ref, k_ref, v_ref, z_ref, gb_ref, gbt_ref, cwq_ref, cwk_ref, cwv_ref, nw_ref,
                     o_ref, st_ref, sb_scr, ob_scr, sk_scr, oq_scr, st_scr, el_scr, s_scr):
    hq = pl.program_id(1)
    nchunk = q_ref.shape[0] // CHUNK
    row = lax.broadcasted_iota(jnp.int32, (CHUNK, CHUNK), 0)
    col = lax.broadcasted_iota(jnp.int32, (CHUNK, CHUNK), 1)
    tri = row >= col
    strict = row > col
    eye = (row == col).astype(f32)
    lane = lax.broadcasted_iota(jnp.int32, (1, LANES), 1)
    rows8 = lax.broadcasted_iota(jnp.int32, (SUBLANES, 1), 0)
    cwq = cwq_ref[...]
    cwk = cwk_ref[...]
    cwv = cwv_ref[...]
    nw = nw_ref[...]

    tile0 = pl.multiple_of((2 * hq) // SUBLANES * SUBLANES, SUBLANES)

    def conv_silu(ref, c, r0, cw):
        x = ref[pl.ds(r0, CHUNK), :]
        p0 = pl.multiple_of(jnp.maximum(r0 - SUBLANES, 0), SUBLANES)
        prev8 = jnp.where(c > 0, ref[pl.ds(p0, SUBLANES), :], 0.0)
        acc = x * cw[GDN_CONV_W - 1:GDN_CONV_W]
        for s in range(1, GDN_CONV_W):
            rolled = pltpu.roll(x, s, 0)
            top = jnp.where(rows8 < s, pltpu.roll(prev8, s, 0), rolled[:SUBLANES])
            shifted = jnp.concatenate([top, rolled[SUBLANES:]], axis=0)
            acc = acc + shifted * cw[GDN_CONV_W - 1 - s:GDN_CONV_W - s]
        return _silu(acc)

    def prepare(i, carry):
        chains = []
        for cc in range(PREP_CHUNKS):
            c = i * PREP_CHUNKS + cc
            r0 = pl.multiple_of(c * CHUNK, CHUNK)
            qn = _l2norm(conv_silu(q_ref, c, r0, cwq)) * GDN_DK ** -0.5
            kn = _l2norm(conv_silu(k_ref, c, r0, cwk))
            vc = conv_silu(v_ref, c, r0, cwv)
            kb = kn.astype(bf16)
            kk = _dot_nt(kb, kb)
            qk = _dot_nt(qn.astype(bf16), kb)
            k_t = kn.T
            gbc = gb_ref[pl.ds(r0, CHUNK), :]
            gbt8 = gbt_ref[pl.ds(tile0, SUBLANES), pl.ds(r0, CHUNK)]
            for hh in range(2):
                hv = 2 * hq + hh
                gc_col = _lane_column(gbc, lane, hv)
                beta_col = _lane_column(gbc, lane, GDN_V_HEADS + hv)
                gc_row = jnp.sum(jnp.where(rows8 == hv % SUBLANES, gbt8, 0.0), axis=0, keepdims=True)
                decay = jnp.where(tri, jnp.exp(jnp.where(tri, gc_col - gc_row, 0.0)), 0.0)
                g_last = gc_row[:, CHUNK - 1:CHUNK]
                el_scr[hh, c] = jnp.broadcast_to(jnp.exp(g_last), (SUBLANES, GDN_DV))
                vh = vc[:, hh * GDN_DV:(hh + 1) * GDN_DV]
                chains.append(dict(
                    hh=hh, c=c,
                    neg_l=jnp.where(strict, -(beta_col * kk * decay), 0.0),
                    vb=(vh * beta_col).astype(bf16),
                    kbg=(kn * (beta_col * jnp.exp(gc_col))).astype(bf16),
                    a=jnp.where(tri, qk * decay, 0.0).astype(bf16),
                    qg=qn * jnp.exp(gc_col),
                    kd=(k_t * jnp.exp(g_last - gc_row)).astype(bf16)))
        invs = _unit_lower_inverses([ch["neg_l"] for ch in chains], eye, row, col)
        inv_bs = [inv.astype(bf16) for inv in invs]
        us = [_dot(inv_b, ch["vb"]).astype(bf16) for ch, inv_b in zip(chains, inv_bs)]
        ws = [_dot(inv_b, ch["kbg"]).astype(bf16) for ch, inv_b in zip(chains, inv_bs)]
        for ch, u, w in zip(chains, us, ws):
            hh, c = ch["hh"], ch["c"]
            sk_scr[hh, c] = _dot(ch["kd"], w).astype(bf16)
            sb_scr[hh, c] = _dot(ch["kd"], u)
            oq_scr[hh, c] = (ch["qg"] - _dot(ch["a"], w)).astype(bf16)
            ob_scr[hh, c] = _dot(ch["a"], u)
        return carry

    def advance(c, carry):
        for hh in range(2):
            state = s_scr[hh]
            state_b = state.astype(bf16)
            st_scr[hh, c] = state_b
            s_scr[hh] = state * el_scr[hh, c][0:1] + (sb_scr[hh, c] - _dot(sk_scr[hh, c], state_b))
        return carry

    def emit(i, carry):
        os_ = []
        for cc in range(OUT_CHUNKS):
            c = i * OUT_CHUNKS + cc
            for hh in range(2):
                os_.append((c, hh, _dot(oq_scr[hh, c], st_scr[hh, c]) + ob_scr[hh, c]))
        for cc in range(OUT_CHUNKS):
            c = i * OUT_CHUNKS + cc
            r0 = pl.multiple_of(c * CHUNK, CHUNK)
            zc = z_ref[pl.ds(r0, CHUNK), :]
            outs = [_gated_rmsnorm(o, nw, zc[:, hh * GDN_DV:(hh + 1) * GDN_DV])
                    for (c2, hh, o) in os_[2 * cc:2 * cc + 2]]
            o_ref[pl.ds(r0, CHUNK), :] = jnp.concatenate(outs, axis=1).astype(o_ref.dtype)
        return carry

    lax.fori_loop(0, nchunk // PREP_CHUNKS, prepare, 0)
    s_scr[...] = jnp.zeros_like(s_scr)
    lax.fori_loop(0, nchunk, advance, 0)
    st_ref[0] = s_scr[...]
    lax.fori_loop(0, nchunk // OUT_CHUNKS, emit, 0)


def _gdn_prompt(p, gb, gbt, conv_w, norm_w, batch, seq, name):
    nqk = GDN_QK_HEADS
    zoff = GDN_CONV_DIM // (2 * GDN_DV)
    nchunk = seq // CHUNK
    tiles_bf = pltpu.VMEM((2, nchunk, CHUNK, CHUNK), bf16)
    tiles_f32 = pltpu.VMEM((2, nchunk, CHUNK, CHUNK), f32)
    return pl.pallas_call(
        _gdn_prompt_body,
        out_shape=(jax.ShapeDtypeStruct((batch * seq, GDN_V_DIM), bf16),
                   jax.ShapeDtypeStruct((batch, GDN_V_HEADS, GDN_DK, GDN_DV), f32)),
        grid=(batch, nqk),
        in_specs=[pl.BlockSpec((seq, GDN_DK), lambda b, h: (b, h)),
                  pl.BlockSpec((seq, GDN_DK), lambda b, h: (b, nqk + h)),
                  pl.BlockSpec((seq, 2 * GDN_DV), lambda b, h: (b, nqk + h)),
                  pl.BlockSpec((seq, 2 * GDN_DV), lambda b, h: (b, zoff + h)),
                  pl.BlockSpec((seq, LANES), lambda b, h: (b, 0)),
                  pl.BlockSpec((LANES, seq), lambda b, h: (0, b)),
                  pl.BlockSpec((GDN_CONV_W, GDN_DK), lambda b, h: (0, h)),
                  pl.BlockSpec((GDN_CONV_W, GDN_DK), lambda b, h: (0, nqk + h)),
                  pl.BlockSpec((GDN_CONV_W, 2 * GDN_DV), lambda b, h: (0, nqk + h)),
                  pl.BlockSpec((1, GDN_DV), lambda b, h: (0, 0))],
        out_specs=(pl.BlockSpec((seq, 2 * GDN_DV), lambda b, h: (b, h)),
                   pl.BlockSpec((1, 2, GDN_DK, GDN_DV), lambda b, h: (b, h, 0, 0))),
        scratch_shapes=[tiles_f32, tiles_f32, tiles_bf, tiles_bf, tiles_bf,
                        pltpu.VMEM((2, nchunk, SUBLANES, GDN_DV), f32),
                        pltpu.VMEM((2, GDN_DK, GDN_DV), f32)],
        compiler_params=_cparams("parallel", "parallel"),
        name=name,
    )(p, p, p, p, gb, gbt, conv_w, conv_w, conv_w, norm_w)


def _gdn_sample_body(q_ref, k_ref, v_ref, z_ref, aq_ref, ak_ref, av_ref, gb_ref, cwq_ref, cwk_ref, cwv_ref,
                     nw_ref, st_ref, *rest):
    o_ref, sto_ref, q_scr, k_scr, v_scr, o_scr = rest[-6:]
    hq = pl.program_id(1)
    rows = q_ref.shape[0]
    nreq = rows // SAMPLE_ROWS
    tok = lax.broadcasted_iota(jnp.int32, (rows, 1), 0) % SAMPLE_ROWS
    lane = lax.broadcasted_iota(jnp.int32, (1, LANES), 1)
    rows8 = lax.broadcasted_iota(jnp.int32, (SAMPLE_ROWS, 1), 0)
    nw = nw_ref[...]
    hist = GDN_CONV_W - 1

    def conv_silu(a, x, cw):
        acc = jnp.zeros_like(x)
        for w in range(GDN_CONV_W):
            ra = a if w == 0 else pltpu.roll(a, rows - w, 0)
            rx = x if w == hist else pltpu.roll(x, hist - w, 0)
            acc = acc + jnp.where(tok + w < hist, ra, rx) * cw[w:w + 1]
        return _silu(acc)

    q_scr[...] = _l2norm(conv_silu(aq_ref[...], q_ref[...], cwq_ref[...])) * GDN_DK ** -0.5
    k_scr[...] = _l2norm(conv_silu(ak_ref[...], k_ref[...], cwk_ref[...]))
    v_scr[...] = conv_silu(av_ref[...], v_ref[...], cwv_ref[...])
    pad = jnp.zeros((LANES - 2 * SAMPLE_ROWS, GDN_DK), f32)

    def requests(i, carry):
        work = []
        for rr in range(REQ_LOCK):
            bi = i * REQ_LOCK + rr
            r0 = pl.multiple_of(bi * SAMPLE_ROWS, SAMPLE_ROWS)
            k8 = k_scr[pl.ds(r0, SAMPLE_ROWS), :]
            q8 = q_scr[pl.ds(r0, SAMPLE_ROWS), :]
            v8 = v_scr[pl.ds(r0, SAMPLE_ROWS), :]
            gb8 = gb_ref[pl.ds(r0, SAMPLE_ROWS), :]
            kq_t = jnp.concatenate([k8, q8, pad], axis=0).T
            for hh in range(2):
                hv = 2 * hq + hh
                work.append(dict(bi=bi, hh=hh, r0=r0, kq_t=kq_t, v=v8[:, hh * GDN_DV:(hh + 1) * GDN_DV],
                                 g=_lane_column(gb8, lane, 2 * GDN_V_HEADS + hv),
                                 beta=_lane_column(gb8, lane, GDN_V_HEADS + hv),
                                 state=st_ref[bi, hh], o=jnp.zeros((SAMPLE_ROWS, GDN_DV), f32)))
        for t in range(GDN_CONV_W):
            for w in work:
                w["state"] = w["state"] * jnp.exp(w["g"][t:t + 1])
            for w in work:
                k_col = w["kq_t"][:, t:t + 1]
                ks = jnp.sum(w["state"] * k_col, axis=0, keepdims=True)
                delta = (w["v"][t:t + 1] - ks) * w["beta"][t:t + 1]
                w["state"] = w["state"] + k_col * delta
            for w in work:
                q_col = w["kq_t"][:, SAMPLE_ROWS + t:SAMPLE_ROWS + t + 1]
                o_t = jnp.sum(w["state"] * q_col, axis=0, keepdims=True)
                w["o"] = jnp.where(rows8 == t, o_t, w["o"])
        for w in work:
            sto_ref[w["bi"], w["hh"]] = w["state"]
        for w0, w1 in zip(work[0::2], work[1::2]):
            z8 = z_ref[pl.ds(w0["r0"], SAMPLE_ROWS), :]
            o_scr[pl.ds(w0["r0"], SAMPLE_ROWS), :] = jnp.concatenate(
                [_gated_rmsnorm(w["o"], nw, z8[:, w["hh"] * GDN_DV:(w["hh"] + 1) * GDN_DV]) for w in (w0, w1)], axis=1)
        return carry

    lax.fori_loop(0, nreq // REQ_LOCK, requests, 0)
    o_ref[...] = o_scr[...].astype(o_ref.dtype)


def _gdn_sample(p, conv_buf, gb, conv_w, norm_w, state, new_state, layer, nreq_blk, name):
    assert GDN_CONV_W == 4 and SAMPLE_ROWS == 8
    nqk = GDN_QK_HEADS
    zoff = GDN_CONV_DIM // (2 * GDN_DV)
    nreq = state.shape[1]
    rows = nreq_blk * SAMPLE_ROWS
    st_spec = pl.BlockSpec((None, nreq_blk, 2, GDN_DK, GDN_DV), lambda i, h: (layer, i, h, 0, 0))
    carried = [] if new_state is None else [new_state]
    return pl.pallas_call(
        _gdn_sample_body,
        out_shape=(jax.ShapeDtypeStruct((nreq * SAMPLE_ROWS, GDN_V_DIM), bf16),
                   jax.ShapeDtypeStruct(state.shape, f32)),
        grid=(nreq // nreq_blk, nqk),
        in_specs=[pl.BlockSpec((rows, GDN_DK), lambda i, h: (i, h)),
                  pl.BlockSpec((rows, GDN_DK), lambda i, h: (i, nqk + h)),
                  pl.BlockSpec((rows, 2 * GDN_DV), lambda i, h: (i, nqk + h)),
                  pl.BlockSpec((rows, 2 * GDN_DV), lambda i, h: (i, zoff + h)),
                  pl.BlockSpec((rows, GDN_DK), lambda i, h: (i, h)),
                  pl.BlockSpec((rows, GDN_DK), lambda i, h: (i, nqk + h)),
                  pl.BlockSpec((rows, 2 * GDN_DV), lambda i, h: (i, nqk + h)),
                  pl.BlockSpec((rows, LANES), lambda i, h: (i, 0)),
                  pl.BlockSpec((GDN_CONV_W, GDN_DK), lambda i, h: (0, h)),
                  pl.BlockSpec((GDN_CONV_W, GDN_DK), lambda i, h: (0, nqk + h)),
                  pl.BlockSpec((GDN_CONV_W, 2 * GDN_DV), lambda i, h: (0, nqk + h)),
                  pl.BlockSpec((1, GDN_DV), lambda i, h: (0, 0)),
                  st_spec] + [pl.BlockSpec(memory_space=pl.ANY)] * len(carried),
        out_specs=(pl.BlockSpec((rows, 2 * GDN_DV), lambda i, h: (i, h)), st_spec),
        scratch_shapes=[pltpu.VMEM((rows, GDN_DK), f32), pltpu.VMEM((rows, GDN_DK), f32),
                        pltpu.VMEM((rows, 2 * GDN_DV), f32), pltpu.VMEM((rows, 2 * GDN_DV), f32)],
        input_output_aliases={13: 1} if carried else {},
        compiler_params=_cparams("parallel", "parallel"),
        name=name,
    )(p, p, p, p, conv_buf, conv_buf, conv_buf, gb, conv_w, conv_w, conv_w, norm_w, state, *carried)


def _swa_prompt_body(*refs):
    qkv_refs = refs[:3 * SWA_GROUPS]
    bias_ref, o_ref, o_scr, l_scr = refs[3 * SWA_GROUPS:]
    seq = o_ref.shape[0]
    lane = lax.broadcasted_iota(jnp.int32, (1, LANES), 1)
    head0 = lane < SWA_HEAD_DIM
    blk = SWA_KEYS
    for g, d in enumerate(SWA_DILATIONS):
        q_ref, k_ref, v_ref = qkv_refs[3 * g:3 * g + 3]
        nb = seq // (d * blk)

        def rows_of(start, d=d):
            return pl.ds(start, blk) if d == 1 else pl.ds(start, blk, stride=d)

        def blocks(i, carry, g=g, d=d, nb=nb, q_ref=q_ref, k_ref=k_ref, v_ref=v_ref, rows_of=rows_of):
            has_prev = nb > 1
            work = []
            for bb in range(ATT_BLOCKS):
                idx = i * ATT_BLOCKS + bb
                r = idx // nb
                b = idx % nb
                cur = rows_of(r + d * blk * b)
                q = q_ref[cur, :] * SWA_HEAD_DIM ** -0.5
                kc = k_ref[cur, :].astype(bf16)
                vc = v_ref[cur, :].astype(bf16)
                kp = vp = no_prev = None
                if has_prev:
                    prev = rows_of(r + d * blk * jnp.maximum(b - 1, 0))
                    no_prev = jnp.where(b == 0, -jnp.inf, 0.0).astype(f32)
                    kp = k_ref[prev, :].astype(bf16)
                    vp = v_ref[prev, :].astype(bf16)
                for hh in range(2):
                    qh = jnp.where(head0 if hh == 0 else ~head0, q, 0.0).astype(bf16)
                    work.append(dict(cur=cur, hh=hh, qh=qh, kc=kc, vc=vc, kp=kp, vp=vp, no_prev=no_prev))
            for w in work:
                w["s_c"] = _dot_nt(w["qh"], w["kc"]) + bias_ref[g, w["hh"]][:, blk:]
            if has_prev:
                for w in work:
                    w["s_p"] = _dot_nt(w["qh"], w["kp"]) + (bias_ref[g, w["hh"]][:, :blk] + w["no_prev"])
            for w in work:
                m = jnp.max(jnp.maximum(w["s_c"], w["s_p"]) if has_prev else w["s_c"], axis=-1, keepdims=True)
                p_c = jnp.exp(w["s_c"] - m)
                w["p_c"] = p_c.astype(bf16)
                if has_prev:
                    p_p = jnp.exp(w["s_p"] - m)
                    w["p_p"] = p_p.astype(bf16)
                    p_c = p_c + p_p
                w["m"], w["den"] = m, jnp.sum(p_c, axis=-1, keepdims=True)
            for w in work:
                w["pv"] = _dot(w["p_c"], w["vc"])
            if has_prev:
                for w in work:
                    w["pv"] = w["pv"] + _dot(w["p_p"], w["vp"])
            for w0, w1 in zip(work[0::2], work[1::2]):
                o_scr[g, w0["cur"], :] = jnp.where(head0, w0["pv"] / w0["den"], w1["pv"] / w1["den"])
                l_scr[g, w0["cur"], :] = jnp.where(head0, w0["m"] + jnp.log(w0["den"]), w1["m"] + jnp.log(w1["den"]))
            return carry

        lax.fori_loop(0, seq // (blk * ATT_BLOCKS), blocks, 0)

    def merge(c, carry):
        rr = pl.ds(pl.multiple_of(c * blk, blk), blk)
        ls = [l_scr[g, rr, :] for g in range(SWA_GROUPS)]
        mx = jnp.maximum(jnp.maximum(ls[0], ls[1]), ls[2])
        ws = [jnp.exp(l - mx) for l in ls]
        num = ws[0] * o_scr[0, rr, :] + ws[1] * o_scr[1, rr, :] + ws[2] * o_scr[2, rr, :]
        o_ref[rr, :] = (num / (ws[0] + ws[1] + ws[2])).astype(o_ref.dtype)
        return carry

    lax.fori_loop(0, seq // blk, merge, 0)


def _swa_prompt(p, bias_t, batch, seq, name):
    npair = SWA_INNER // LANES
    in_specs = []
    for g in range(SWA_GROUPS):
        for c in range(3):
            off = (g * 3 + c) * npair
            in_specs.append(pl.BlockSpec((seq, LANES), lambda b, h, off=off: (b, off + h)))
    in_specs.append(pl.BlockSpec((SWA_GROUPS, 2, SWA_KEYS, 2 * SWA_KEYS), lambda b, h: (0, h, 0, 0)))
    return pl.pallas_call(
        _swa_prompt_body,
        out_shape=jax.ShapeDtypeStruct((batch * seq, SWA_INNER), bf16),
        grid=(batch, npair),
        in_specs=in_specs,
        out_specs=pl.BlockSpec((seq, LANES), lambda b, h: (b, h)),
        scratch_shapes=[pltpu.VMEM((SWA_GROUPS, seq, LANES), f32), pltpu.VMEM((SWA_GROUPS, seq, LANES), f32)],
        compiler_params=_cparams("parallel", "parallel"),
        name=name,
    )(*([p] * (3 * SWA_GROUPS)), bias_t)


def _swa_sample_body(*refs):
    qkv_refs = refs[:3 * SWA_GROUPS]
    cache_refs = refs[3 * SWA_GROUPS:4 * SWA_GROUPS]
    tb_refs = refs[4 * SWA_GROUPS:5 * SWA_GROUPS]
    tn_ref, o_ref = refs[5 * SWA_GROUPS:]
    nhead = cache_refs[0].shape[1]
    heads = [slice(h * SAMPLE_ROWS, (h + 1) * SAMPLE_ROWS) for h in range(nhead)]

    def by_head(ref):
        x = ref[...]
        return jnp.concatenate([x[:, h * SWA_HEAD_DIM:(h + 1) * SWA_HEAD_DIM] for h in range(nhead)], axis=0)

    qs, s_cache, s_new = [], [], []
    for g in range(SWA_GROUPS):
        q = (by_head(qkv_refs[3 * g]) * SWA_HEAD_DIM ** -0.5).astype(bf16)
        qs.append(q)
        s_cache.append(jnp.concatenate(
            [_dot(q[heads[h]], cache_refs[g][0, h].astype(bf16)) for h in range(nhead)], axis=0) + tb_refs[g][...])
    for g in range(SWA_GROUPS):
        k_new = by_head(qkv_refs[3 * g + 1]).astype(bf16)
        s_new.append(_dot_nt(qs[g], k_new) + tn_ref[g])
    ms, dens, ps, pns = [], [], [], []
    for g in range(SWA_GROUPS):
        m = jnp.maximum(jnp.max(s_cache[g], axis=-1, keepdims=True), jnp.max(s_new[g], axis=-1, keepdims=True))
        p = jnp.exp(s_cache[g] - m)
        pn = jnp.exp(s_new[g] - m)
        ms.append(m)
        dens.append(jnp.sum(p, axis=-1, keepdims=True) + jnp.sum(pn, axis=-1, keepdims=True))
        ps.append(p.astype(bf16))
        pns.append(pn.astype(bf16))
    pvs = []
    for g in range(SWA_GROUPS):
        pvs.append(jnp.concatenate(
            [_dot_nt(ps[g][heads[h]], cache_refs[g][1, h].astype(bf16)) for h in range(nhead)], axis=0))
    for g in range(SWA_GROUPS):
        v_new = by_head(qkv_refs[3 * g + 2]).astype(bf16)
        pvs[g] = pvs[g] + _dot(pns[g], v_new)
    o_g = [pv / den for pv, den in zip(pvs, dens)]
    l_g = [m + jnp.log(den) for m, den in zip(ms, dens)]
    mx = jnp.maximum(jnp.maximum(l_g[0], l_g[1]), l_g[2])
    ws = [jnp.exp(l - mx) for l in l_g]
    merged = (ws[0] * o_g[0] + ws[1] * o_g[1] + ws[2] * o_g[2]) / (ws[0] + ws[1] + ws[2])
    o_ref[...] = jnp.concatenate([merged[heads[h]] for h in range(nhead)], axis=1)


def _swa_sample(p, caches_t, layer, tb, tn, name):
    nreq = p.shape[0] // SAMPLE_ROWS
    nhalf = 2
    nhead = SWA_HEADS // nhalf
    nrow = nhead * SAMPLE_ROWS
    width = nhead * SWA_HEAD_DIM
    in_specs = [pl.BlockSpec((SAMPLE_ROWS, width), lambda h, b, blk=blk: (b, nhalf * blk + h))
                for blk in range(3 * SWA_GROUPS)]
    for c in caches_t:
        in_specs.append(pl.BlockSpec((None, None, 2, nhead, SWA_HEAD_DIM, c.shape[-1]),
                                     lambda h, b: (layer, b, 0, h, 0, 0)))
    for t in tb:
        in_specs.append(pl.BlockSpec((nrow, t.shape[-1]), lambda h, b: (h, 0)))
    in_specs.append(pl.BlockSpec((None, SWA_GROUPS, nrow, nrow), lambda h, b: (h, 0, 0, 0)))
    return pl.pallas_call(
        _swa_sample_body,
        out_shape=jax.ShapeDtypeStruct((nreq * SAMPLE_ROWS, SWA_INNER), f32),
        grid=(nhalf, nreq),
        in_specs=in_specs,
        out_specs=pl.BlockSpec((SAMPLE_ROWS, width), lambda h, b: (b, h)),
        compiler_params=_cparams("parallel", "parallel"),
        name=name,
    )(*([p] * (3 * SWA_GROUPS)), *caches_t, *tb, tn)


def _window_t_body(*refs):
    o_ref = refs[-1]
    layer = pl.program_id(0)
    for n, p_ref in enumerate(refs[:-1]):
        @pl.when(layer == n)
        def _(p_ref=p_ref):
            o_ref[...] = p_ref[...].T


def _window_t(ps, g, batch, seq, keep, name):
    tk = min(keep, 512)
    first = (seq - keep) // tk
    per_req = seq // tk
    col0 = 3 * g + 1

    def rows_of(n):
        return lambda l, b, c, i: (jnp.where(l == n, b * per_req + first + i, first), col0 + jnp.where(l == n, c, 0))

    return pl.pallas_call(
        _window_t_body,
        out_shape=jax.ShapeDtypeStruct((len(ps), batch, 2, SWA_INNER, keep), f32),
        grid=(len(ps), batch, 2, keep // tk),
        in_specs=[pl.BlockSpec((tk, SWA_INNER), rows_of(n)) for n in range(len(ps))],
        out_specs=pl.BlockSpec((None, None, None, SWA_INNER, tk), lambda l, b, c, i: (l, b, c, 0, i)),
        compiler_params=_cparams("parallel", "parallel", "parallel", "parallel"),
        name=name,
    )(*ps)


def _t5_bucket(dist):
    max_exact = N_BUCKETS // 2
    dd = jnp.maximum(dist, 1).astype(f32)
    large = max_exact + (jnp.log(dd / max_exact) / math.log(MAX_DISTANCE / max_exact)
                         * (N_BUCKETS - max_exact)).astype(jnp.int32)
    return jnp.where(dist < max_exact, dist, jnp.minimum(large, N_BUCKETS - 1))


def _select_rows(table, idx):
    onehot = (idx[..., None] == jnp.arange(table.shape[0])).astype(f32)
    return jnp.einsum("...n,nc->...c", onehot, table, precision=lax.Precision.HIGHEST)


def _bias_tables(rel_bias, ntok, cache_lens):
    neg = -jnp.inf
    i = jnp.arange(SWA_KEYS)[:, None]
    jj = jnp.arange(2 * SWA_KEYS)[None, :]
    steps = SWA_KEYS + i - jj
    band = (steps >= 0) & (steps <= SWA_KEYS)
    t_row = jnp.where(jnp.arange(SAMPLE_ROWS) < ntok, jnp.arange(SAMPLE_ROWS), 0)[:, None]
    t_new = jnp.arange(SAMPLE_ROWS)[None, :]
    same_head = jnp.eye(SWA_HEADS, dtype=bool)[:, None, :, None]
    nhalf = SWA_HEADS // 2 * SAMPLE_ROWS
    prompt_t, cached_t, new_t = [], [], []
    for g, d in enumerate(SWA_DILATIONS):
        dist = jnp.arange(SWA_KEYS + 1, dtype=jnp.int32) * d
        bias_g = _select_rows(rel_bias[:, g * SWA_HEADS:(g + 1) * SWA_HEADS], _t5_bucket(dist))
        toeplitz = _select_rows(bias_g, jnp.clip(steps, 0, SWA_KEYS))
        prompt_t.append(jnp.where(band[None], jnp.transpose(toeplitz, (2, 0, 1)), neg))
        length = cache_lens[g]
        back = length + t_row - jnp.arange(length)[None, :]
        ok = (back % d == 0) & (back >= d) & (back <= SWA_KEYS * d)
        vals = _select_rows(bias_g, jnp.clip(back // d, 0, SWA_KEYS))
        cached = jnp.where(ok[None], jnp.transpose(vals, (2, 0, 1)), neg)
        cached_t.append(cached.reshape(SWA_HEADS * SAMPLE_ROWS, length))
        back = t_row - t_new
        ok = (back % d == 0) & (back >= 0) & (t_new < ntok)
        vals = _select_rows(bias_g, jnp.clip(back // d, 0, SWA_KEYS))
        fresh = jnp.where(ok[None], jnp.transpose(vals, (2, 0, 1)), neg)
        full = jnp.where(same_head, fresh[:, :, None, :], neg).reshape(2 * nhalf, 2 * nhalf)
        new_t.append(jnp.stack([full[:nhalf, :nhalf], full[nhalf:, nhalf:]]))
    return jnp.stack(prompt_t), cached_t, jnp.stack(new_t, axis=1)


def kernel(x_prompt, x_sample, state_gdn, state_conv, cache_win0, cache_win1, cache_win2, w_in_gdn, conv_w_gdn,
           a_log_gdn, dt_bias_gdn, norm_gdn, w_out_gdn, w_in_swa, w_out_swa, rel_bias, w_ffn_in, w_ffn_out,
           ln_mix_g, ln_mix_b, ln_ffn_g, ln_ffn_b):
    batch, seq, _ = x_prompt.shape
    nreq, ntok, _ = x_sample.shape
    caches = (cache_win0, cache_win1, cache_win2)
    mp = batch * seq
    tm_p = 1024
    tm_s = nreq * SAMPLE_ROWS

    xp = x_prompt.reshape(mp, D_MODEL)
    xs = jnp.pad(x_sample, ((0, 0), (0, SAMPLE_ROWS - ntok), (0, 0))).reshape(tm_s, D_MODEL)
    caches_t = tuple(jnp.transpose(c, (0, 1, 3, 4, 5, 2)) for c in caches)
    bias_p, bias_s, bias_new = _bias_tables(rel_bias, ntok, [c.shape[2] for c in caches])

    gdn_p, conv_p, conv_s = [], [], []
    gdn_s = jnp.zeros(state_gdn.shape, f32)
    swa_pp = []
    win_s = [[] for _ in range(SWA_GROUPS)]
    for i in range(DEPTH):
        j = i // N_MIXERS
        ln_g, ln_b = ln_mix_g[i][None], ln_mix_b[i][None]
        if i % N_MIXERS == 0:
            w = w_in_gdn[j]
            w_qkvz = w[:, :GDN_QKVZ_DIM].astype(bf16)
            w_a = w[:, GDN_QKVZ_DIM:GDN_QKVZ_DIM + GDN_V_HEADS]
            w_b = w[:, GDN_QKVZ_DIM + GDN_V_HEADS:]
            fill = jnp.zeros((D_MODEL, LANES - 3 * GDN_V_HEADS), f32)
            w_ab = jnp.concatenate([w_a, w_b, w_a, fill], axis=1).astype(bf16)
            zpad = jnp.zeros((LANES - 3 * GDN_V_HEADS,), f32)
            zh = jnp.zeros((GDN_V_HEADS,), f32)
            alog = jnp.concatenate([a_log_gdn[j], zh, a_log_gdn[j], zpad])[None]
            dtb = jnp.concatenate([dt_bias_gdn[j], zh, dt_bias_gdn[j], zpad])[None]
            conv_w = conv_w_gdn[j]
            norm_w = norm_gdn[j][None]
            w_out = w_out_gdn[j].astype(bf16)

            pp = _proj_rows(xp, w_qkvz, PROJ_ROWS, f"gdn_in_p{j}")
            ps = _proj(xs, w_qkvz, tm_s, 1024, f"gdn_in_s{j}")
            gb_p, gbt_p = _gates(xp, w_ab, alog, dtb, tm_p, f"gdn_gates_p{j}")
            gb_s, _ = _gates(xs, w_ab, alog, dtb, tm_s, f"gdn_gates_s{j}")
            o_p, st_p = _gdn_prompt(pp, gb_p, gbt_p, conv_w, norm_w, batch, seq, f"gdn_prompt{j}")
            cbuf = jnp.pad(state_conv[j], ((0, 0), (0, SAMPLE_ROWS - (GDN_CONV_W - 1)), (0, 0)))
            o_s, gdn_s = _gdn_sample(ps, cbuf.reshape(tm_s, GDN_CONV_DIM), gb_s, conv_w, norm_w, state_gdn, gdn_s,
                                     j, 8, f"gdn_sample{j}")
            xp = _mm_ln(o_p, w_out, xp, ln_g, ln_b, tm_p, f"gdn_out_p{j}")
            xs = _mm_ln(o_s, w_out, xs, ln_g, ln_b, tm_s, f"gdn_out_s{j}")
            gdn_p.append(st_p)
            conv_p.append(pp.reshape(batch, seq, GDN_QKVZ_DIM)[:, seq - (GDN_CONV_W - 1):, :GDN_CONV_DIM])
            conv_s.append(ps.reshape(nreq, SAMPLE_ROWS, GDN_QKVZ_DIM)[:, ntok - (GDN_CONV_W - 1):ntok, :GDN_CONV_DIM])
        else:
            w_in = w_in_swa[j].astype(bf16)
            w_out = w_out_swa[j].astype(bf16)
            pp = _proj_rows(xp, w_in, PROJ_ROWS, f"swa_in_p{j}")
            ps = _proj(xs, w_in, tm_s, 1024, f"swa_in_s{j}")
            a_p = _swa_prompt(pp, bias_p, batch, seq, f"swa_prompt{j}")
            a_s = _swa_sample(ps, caches_t, j, bias_s, bias_new, f"swa_sample{j}")
            ps3 = ps.reshape(nreq, SAMPLE_ROWS, SWA_IN_DIM)
            xp = _mm_ln(a_p, w_out, xp, ln_g, ln_b, tm_p, f"swa_out_p{j}")
            xs = _mm_ln(a_s, w_out, xs, ln_g, ln_b, tm_s, f"swa_out_s{j}")
            swa_pp.append(pp)
            for g in range(SWA_GROUPS):
                lo, hi = (3 * g + 1) * SWA_INNER, (3 * g + 3) * SWA_INNER
                win_s[g].append(ps3[:, :ntok, lo:hi].reshape(nreq, ntok, 2, SWA_HEADS, SWA_HEAD_DIM))
        w_fi = w_ffn_in[i].astype(bf16)
        w_fo = w_ffn_out[i].astype(bf16)
        xp = _ffn(xp, w_fi, w_fo, ln_ffn_g[i][None], ln_ffn_b[i][None], tm_p, f"ffn_p{i}")
        xs = _ffn(xs, w_fi, w_fo, ln_ffn_g[i][None], ln_ffn_b[i][None], tm_s, f"ffn_s{i}")

    win_p = [_window_t(swa_pp, g, batch, seq, min(SWA_KEYS * SWA_DILATIONS[g], seq), f"win_p{g}")
             for g in range(SWA_GROUPS)]
    y_p = xp.reshape(batch, seq, D_MODEL)
    y_s = xs.reshape(nreq, SAMPLE_ROWS, D_MODEL)[:, :ntok]
    return (y_p, y_s, jnp.stack(gdn_p), jnp.stack(conv_p),
            *[jnp.transpose(w.reshape(w.shape[:3] + (SWA_HEADS, SWA_HEAD_DIM, w.shape[-1])), (0, 1, 5, 2, 3, 4))
              for w in win_p],
            gdn_s, jnp.stack(conv_s),
            jnp.stack(win_s[0]), jnp.stack(win_s[1]), jnp.stack(win_s[2]))
```
